```python
import jax, jax.numpy as jnp
from jax import lax
import numpy as np

D_MODEL = 4096
BATCH = 4
SEQ = 2048
DEPTH = 1

D_MIX = D_MODEL
GLA_HEADS = 8
GLA_DK = D_MODEL // 32
GLA_DV = D_MODEL // 16
GLA_GATE_RANK = 16
GLA_GATE_NORMALIZER = 16.0
GLA_CHUNK = 64
MOBA_HEADS = 16
MOBA_HD = D_MODEL // 32
MOBA_BLOCK = 256
MOBA_TOPK = 3
MOBA_QCHUNK = 16
ROPE_THETA = 500000.0
ROPE_DIMS = MOBA_HD // 4
PEER_HEADS = 8
PEER_NKEYS = 128
PEER_EXPERTS = PEER_NKEYS * PEER_NKEYS
PEER_DQ = 256
PEER_TOPK = 16
PEER_TCHUNK = 128
EPS = 1e-6

GLA_QK_W = GLA_HEADS * GLA_DK
GLA_V_W = GLA_HEADS * GLA_DV
MOBA_W = MOBA_HEADS * MOBA_HD
IN_COLS = 2 * GLA_QK_W + 2 * GLA_V_W + GLA_GATE_RANK + 3 * MOBA_W
SPLIT_POINTS = (GLA_QK_W, 2 * GLA_QK_W, 2 * GLA_QK_W + GLA_V_W, 2 * GLA_QK_W + 2 * GLA_V_W,
                2 * GLA_QK_W + 2 * GLA_V_W + GLA_GATE_RANK,
                2 * GLA_QK_W + 2 * GLA_V_W + GLA_GATE_RANK + MOBA_W,
                2 * GLA_QK_W + 2 * GLA_V_W + GLA_GATE_RANK + 2 * MOBA_W)

kernel_name = "hymba_gla_moba_peer_layer"


def rmsnorm(x, g):
    xf = x.astype(jnp.float32)
    r = lax.rsqrt(jnp.mean(xf * xf, axis=-1, keepdims=True) + EPS)
    return (xf * r).astype(x.dtype) * g


def partial_rope(x, pos):
    half = ROPE_DIMS // 2
    inv = ROPE_THETA ** (-jnp.arange(half, dtype=jnp.float32) / half)
    ang = pos.astype(jnp.float32)[:, None] * inv[None, :]
    cos = jnp.cos(ang).astype(x.dtype)
    sin = jnp.sin(ang).astype(x.dtype)
    x1, x2, xp = x[..., :half], x[..., half:ROPE_DIMS], x[..., ROPE_DIMS:]
    return jnp.concatenate([x1 * cos - x2 * sin, x2 * cos + x1 * sin, xp], axis=-1)


def gla_mixer(q, k, v, g_out, gate_lr, w_gate_up, b_gate, gla_norm_g):
    B, T, _ = q.shape
    H, dk, dv, C = GLA_HEADS, GLA_DK, GLA_DV, GLA_CHUNK
    N = T // C
    f32 = jnp.float32
    log_a = jax.nn.log_sigmoid((gate_lr @ w_gate_up + b_gate).astype(f32)) / GLA_GATE_NORMALIZER

    def chunked(t, d):
        return t.reshape(B, N, C, H, d).transpose(0, 3, 1, 2, 4)

    qc = chunked(q, dk).astype(f32) * (dk ** -0.5)
    kc = chunked(k, dk).astype(f32)
    vc = chunked(v, dv).astype(f32)
    bc = jnp.cumsum(chunked(log_a, dk), axis=3)
    b_last = bc[..., -1:, :]
    q_dec = qc * jnp.exp(bc)
    k_intra = kc * jnp.exp(-bc)
    k_state = kc * jnp.exp(b_last - bc)
    causal = jnp.tril(jnp.ones((C, C), dtype=bool))
    att = jnp.where(causal, jnp.einsum('bhncd,bhnsd->bhncs', q_dec, k_intra), 0.0)
    o_intra = jnp.einsum('bhncs,bhnse->bhnce', att, vc)
    chunk_update = jnp.einsum('bhnsd,bhnse->nbhde', k_state, vc)
    chunk_decay = jnp.exp(b_last[..., 0, :]).transpose(2, 0, 1, 3)

    def step(S, inp):
        dec, upd = inp
        return dec[..., None] * S + upd, S

    S0 = jnp.zeros((B, H, dk, dv), f32)
    _, S_prev = lax.scan(step, S0, (chunk_decay, chunk_update))
    o_inter = jnp.einsum('bhncd,nbhde->bhnce', q_dec, S_prev)
    o = (o_intra + o_inter).transpose(0, 2, 3, 1, 4).reshape(B, T, H, dv)
    o = rmsnorm(o, gla_norm_g) * jax.nn.silu(g_out.reshape(B, T, H, dv).astype(f32))
    return o.reshape(B, T, H * dv).astype(v.dtype)


def moba_mixer(q, k, v, q_norm_g, k_norm_g):
    B, T, _ = q.shape
    H, hd, L, QC = MOBA_HEADS, MOBA_HD, MOBA_BLOCK, MOBA_QCHUNK
    Tp = -(-T // L) * L
    nb = Tp // L
    n_sel = min(MOBA_TOPK, nb)
    f32 = jnp.float32

    def heads(t):
        t = t.reshape(B, T, H, hd)
        t = jnp.pad(t, ((0, 0), (0, Tp - T), (0, 0), (0, 0)))
        return t.transpose(0, 2, 1, 3)

    pos = jnp.arange(Tp)
    qh = partial_rope(rmsnorm(heads(q), q_norm_g), pos) * (hd ** -0.5)
    kh = partial_rope(rmsnorm(heads(k), k_norm_g), pos)
    vh = heads(v)
    kb = kh.reshape(B, H, nb, L, hd)
    vb = vh.reshape(B, H, nb, L, hd)
    k_mean = jnp.mean(kb.astype(f32), axis=3)
    bi = jnp.arange(B)[:, None, None, None]
    hi = jnp.arange(H)[None, :, None, None]
    slot = jnp.arange(n_sel)

    def chunk_fn(c):
        start = c * QC
        blk = start // L
        qc = lax.dynamic_slice_in_dim(qh, start, QC, axis=2)
        gate = jnp.einsum('bhqd,bhnd->bhqn', qc.astype(f32), k_mean)
        gate = jnp.where(jnp.arange(nb) < blk, gate, -jnp.inf)
        _, sel = lax.top_k(gate, n_sel)
        slot_ok = slot < blk
        k_sel = kb[bi, hi, sel]
        v_sel = vb[bi, hi, sel]
        s_sel = jnp.einsum('bhqd,bhqskd->bhqsk', qc, k_sel, preferred_element_type=f32)
        s_sel = jnp.where(slot_ok[:, None], s_sel, -jnp.inf).reshape(B, H, QC, n_sel * L)
        k_own = lax.dynamic_index_in_dim(kb, blk, axis=2, keepdims=False)
        v_own = lax.dynamic_index_in_dim(vb, blk, axis=2, keepdims=False)
        qpos = start + jnp.arange(QC)
        kpos = blk * L + jnp.arange(L)
        s_own = jnp.einsum('bhqd,bhkd->bhqk', qc, k_own, preferred_element_type=f32)
        s_own = jnp.where(kpos[None, :] <= qpos[:, None], s_own, -jnp.inf)
        p = jax.nn.softmax(jnp.concatenate([s_own, s_sel], axis=-1), axis=-1).astype(v.dtype)
        p_own = p[..., :L]
        p_sel = p[..., L:].reshape(B, H, QC, n_sel, L)
        return (jnp.einsum('bhqk,bhkd->bhqd', p_own, v_own)
                + jnp.einsum('bhqsk,bhqskd->bhqd', p_sel, v_sel))

    out = lax.map(chunk_fn, jnp.arange(Tp // QC))
    out = out.transpose(1, 2, 0, 3, 4).reshape(B, H, Tp, hd)[:, :, :T]
    return out.transpose(0, 2, 1, 3).reshape(B, T, H * hd)


def peer_ffn(x, w_pq, peer_keys, peer_u, peer_v):
    B, T, D = x.shape
    H, K, NK, TC = PEER_HEADS, PEER_TOPK, PEER_NKEYS, PEER_TCHUNK
    N = B * T
    f32 = jnp.float32
    xt = x.reshape(N, D)
    q = (xt @ w_pq).reshape(N, H, 2, PEER_DQ // 2)
    s = jnp.einsum('thpd,hpkd->thpk', q.astype(f32), peer_keys.astype(f32))
    s1, i1 = lax.top_k(s[:, :, 0], K)
    s2, i2 = lax.top_k(s[:, :, 1], K)
    cand = (s1[..., :, None] + s2[..., None, :]).reshape(N, H, K * K)
    top_s, top_c = lax.top_k(cand, K)
    expert = (jnp.take_along_axis(i1, top_c // K, axis=-1) * NK
              + jnp.take_along_axis(i2, top_c % K, axis=-1))
    g = jax.nn.softmax(top_s, axis=-1)
    nc = N // TC

    def chunk_fn(inp):
        xc, ec, gc = inp
        ef = ec.reshape(TC, H * K)
        u = peer_u[ef]
        h = jax.nn.gelu(jnp.einsum('td,ted->te', xc, u), approximate=False)
        w = gc.reshape(TC, H * K).astype(xc.dtype) * h
        return jnp.einsum('te,ted->td', w, peer_v[ef])

    out = lax.map(chunk_fn, (xt.reshape(nc, TC, D), expert.reshape(nc, TC, H, K),
                             g.reshape(nc, TC, H, K)))
    return out.reshape(B, T, D)


def setup_inputs(seed: int = 0) -> dict:
    key = jax.random.key(seed)
    ks = jax.random.split(key, 16)
    f32 = jnp.float32

    def w(k, shape, fan_in):
        return jax.random.normal(k, shape, f32) * (fan_in ** -0.5)

    def gain(k, shape):
        return 1.0 + 0.02 * jax.random.normal(k, shape, f32)

    return {
        "x": jax.random.normal(ks[0], (BATCH, SEQ, D_MODEL), f32),
        "norm_mix_g": gain(ks[1], (DEPTH, D_MODEL)),
        "w_in": w(ks[2], (DEPTH, D_MODEL, IN_COLS), D_MODEL),
        "w_gate_up": w(ks[3], (DEPTH, GLA_GATE_RANK, GLA_QK_W), GLA_GATE_RANK),
        "b_gate": 0.1 * jax.random.normal(ks[4], (DEPTH, GLA_QK_W), f32),
        "gla_norm_g": gain(ks[5], (DEPTH, GLA_DV)),
        "q_norm_g": gain(ks[6], (DEPTH, MOBA_HD)),
        "k_norm_g": gain(ks[7], (DEPTH, MOBA_HD)),
        "w_out": w(ks[8], (DEPTH, D_MIX, D_MODEL), D_MIX),
        "norm_ffn_g": gain(ks[9], (DEPTH, D_MODEL)),
        "peer_wq": w(ks[10], (DEPTH, D_MODEL, PEER_HEADS * PEER_DQ), D_MODEL),
        "peer_keys": w(ks[11], (DEPTH, PEER_HEADS, 2, PEER_NKEYS, PEER_DQ // 2), PEER_DQ // 2),
        "peer_u": w(ks[12], (DEPTH, PEER_EXPERTS, D_MODEL), D_MODEL),
        "peer_v": w(ks[13], (DEPTH, PEER_EXPERTS, D_MODEL), PEER_HEADS * PEER_TOPK),
    }


def reference(x, norm_mix_g, w_in, w_gate_up, b_gate, gla_norm_g, q_norm_g, k_norm_g,
              w_out, norm_ffn_g, peer_wq, peer_keys, peer_u, peer_v):
    for l in range(DEPTH):
        h = rmsnorm(x, norm_mix_g[l])
        proj = h @ w_in[l]
        gq, gk, gv, gg, glr, mq, mk, mv = jnp.split(proj, SPLIT_POINTS, axis=-1)
        o_gla = gla_mixer(gq, gk, gv, gg, glr, w_gate_up[l], b_gate[l], gla_norm_g[l])
        o_moba = moba_mixer(mq, mk, mv, q_norm_g[l], k_norm_g[l])
        x = x + jnp.concatenate([o_gla, o_moba], axis=-1) @ w_out[l]
        x = x + peer_ffn(rmsnorm(x, norm_ffn_g[l]), peer_wq[l], peer_keys[l], peer_u[l], peer_v[l])
    return x
```

```python
import functools
import math

import jax
import jax.numpy as jnp
from jax import lax
from jax.experimental import pallas as pl
from jax.experimental.pallas import tpu as pltpu

GLA_HEADS = 8
GLA_DK = 128
GLA_DV = 256
GLA_GATE_RANK = 16
GLA_GATE_NORMALIZER = 16.0
GLA_CHUNK = 64
MOBA_HEADS = 16
MOBA_HD = 128
MOBA_BLOCK = 256
MOBA_TOPK = 3
ROPE_THETA = 500000.0
ROPE_DIMS = MOBA_HD // 4
PEER_HEADS = 8
PEER_NKEYS = 128
PEER_DQ = 256
PEER_TOPK = 16
EPS = 1e-6

LANES = 128
SUBLANES = 8
VMEM_LIMIT_BYTES = 60 * 1024 * 1024

_HI = lax.Precision.HIGHEST
_NEG_INF = float("-inf")


def _dot(a, b, precision=None):
    return lax.dot_general(a, b, (((1,), (0,)), ((), ())), precision=precision,
                           preferred_element_type=jnp.float32)


def _dot_nt(a, b, precision=None):
    return lax.dot_general(a, b, (((1,), (1,)), ((), ())), precision=precision,
                           preferred_element_type=jnp.float32)


def _dot_tn(a, b, precision=None):
    return lax.dot_general(a, b, (((0,), (0,)), ((), ())), precision=precision,
                           preferred_element_type=jnp.float32)


def _tile(n, pref):
    t = min(n, pref)
    while n % t:
        t //= 2
    return t


def _params(*sem):
    return pltpu.CompilerParams(dimension_semantics=sem, vmem_limit_bytes=VMEM_LIMIT_BYTES)


def _in_proj_kernel(x_ref, g_ref, w_ref, wlr_ref, o_ref, olr_ref, hb_ref):
    @pl.when(pl.program_id(1) == 0)
    def _():
        x = x_ref[...]
        r = lax.rsqrt(jnp.mean(x * x, axis=-1, keepdims=True) + EPS)
        hb = ((x * r) * g_ref[...]).astype(jnp.bfloat16)
        hb_ref[...] = hb
        olr_ref[...] = _dot(hb, wlr_ref[...])

    o_ref[...] = _dot(hb_ref[...], w_ref[...])


def _in_proj(x2, g, w_main, w_lr):
    n, d = x2.shape
    cols = w_main.shape[1]
    tm = _tile(n, 512)
    tn = _tile(cols, 1024)
    return pl.pallas_call(
        _in_proj_kernel,
        grid=(n // tm, cols // tn),
        in_specs=[
            pl.BlockSpec((tm, d), lambda i, j: (i, 0)),
            pl.BlockSpec((1, d), lambda i, j: (0, 0)),
            pl.BlockSpec((d, tn), lambda i, j: (0, j)),
            pl.BlockSpec((d, LANES), lambda i, j: (0, 0)),
        ],
        out_specs=[
            pl.BlockSpec((tm, tn), lambda i, j: (i, j)),
            pl.BlockSpec((tm, LANES), lambda i, j: (i, 0)),
        ],
        out_shape=[
            jax.ShapeDtypeStruct((n, cols), jnp.float32),
            jax.ShapeDtypeStruct((n, LANES), jnp.float32),
        ],
        scratch_shapes=[pltpu.VMEM((tm, d), jnp.bfloat16)],
        compiler_params=_params("parallel", "arbitrary"),
        name="in_proj",
    )(x2, g, w_main, w_lr)


def _gla_kernel(q_ref, k_ref, v_ref, go_ref, glr_ref, wup_ref, bg_ref, ng_ref, o_ref, st_ref):
    t, dk = q_ref.shape
    dv = v_ref.shape[1]
    c = GLA_CHUNK
    scale = dk ** -0.5
    row = lax.broadcasted_iota(jnp.int32, (c, c), 0)
    col = lax.broadcasted_iota(jnp.int32, (c, c), 1)
    causal = row >= col
    tril = causal.astype(jnp.float32)
    st_ref[...] = jnp.zeros_like(st_ref)

    def body(ci, carry):
        sl = pl.ds(pl.multiple_of(ci * c, c), c)
        z = _dot(glr_ref[sl, :], wup_ref[...], _HI) + bg_ref[...]
        log_a = (jnp.minimum(z, 0.0) - jnp.log1p(jnp.exp(-jnp.abs(z)))) / GLA_GATE_NORMALIZER
        bc = _dot(tril, log_a, _HI)
        b_last = bc[c - 1:c, :]
        q = q_ref[sl, :]
        k = k_ref[sl, :]
        v = v_ref[sl, :].astype(jnp.bfloat16)
        q_dec = (q * scale * jnp.exp(bc)).astype(jnp.bfloat16)
        k_intra = (k * jnp.exp(-bc)).astype(jnp.bfloat16)
        k_state = (k * jnp.exp(b_last - bc)).astype(jnp.bfloat16)
        att = jnp.where(causal, _dot_nt(q_dec, k_intra), 0.0).astype(jnp.bfloat16)
        st = st_ref[...]
        o = _dot(att, v) + _dot_nt(q_dec, st.astype(jnp.bfloat16))
        st_ref[...] = jnp.exp(b_last) * st + _dot_tn(v, k_state)
        r = lax.rsqrt(jnp.mean(o * o, axis=-1, keepdims=True) + EPS)
        go = go_ref[sl, :]
        o_ref[sl, :] = (((o * r) * ng_ref[...]) * (go * jax.nn.sigmoid(go))).astype(o_ref.dtype)
        return carry

    lax.fori_loop(0, t // c, body, 0)


def _gla(proj, glr, wup_pad, b_gate, norm_g, batch, t):
    n = proj.shape[0]
    h, dk, dv = GLA_HEADS, GLA_DK, GLA_DV
    qk_w = h * dk
    kb = qk_w // dk
    vb = (2 * qk_w) // dv
    gb = (2 * qk_w + h * dv) // dv
    return pl.pallas_call(
        _gla_kernel,
        grid=(batch, h),
        in_specs=[
            pl.BlockSpec((t, dk), lambda b, i: (b, i)),
            pl.BlockSpec((t, dk), lambda b, i: (b, kb + i)),
            pl.BlockSpec((t, dv), lambda b, i: (b, vb + i)),
            pl.BlockSpec((t, dv), lambda b, i: (b, gb + i)),
            pl.BlockSpec((t, LANES), lambda b, i: (b, 0)),
            pl.BlockSpec((LANES, dk), lambda b, i: (0, i)),
            pl.BlockSpec((1, dk), lambda b, i: (0, i)),
            pl.BlockSpec((1, dv), lambda b, i: (0, 0)),
        ],
        out_specs=pl.BlockSpec((t, dv), lambda b, i: (b, i)),
        out_shape=jax.ShapeDtypeStruct((n, h * dv), jnp.bfloat16),
        scratch_shapes=[pltpu.VMEM((dv, dk), jnp.float32)],
        compiler_params=_params("parallel", "parallel"),
        name="gla",
    )(proj, proj, proj, proj, glr, wup_pad, b_gate, norm_g)


def _moba_kernel(q_ref, k_ref, v_ref, qg_ref, kg_ref, cos_ref, sa_ref, sb_ref, o_ref,
                 qh_ref, kh_ref, vb_ref, km_ref, sel_ref):
    t, hd = q_ref.shape
    blk = MOBA_BLOCK
    nb = t // blk
    half = ROPE_DIMS // 2

    def prep(x, g, scale):
        r = lax.rsqrt(jnp.mean(x * x, axis=-1, keepdims=True) + EPS)
        xn = (x * r) * g
        y = (xn * cos_ref[...] + pltpu.roll(xn, half, 1) * sa_ref[...]
             + pltpu.roll(xn, hd - half, 1) * sb_ref[...])
        return y * scale

    qh = prep(q_ref[...], qg_ref[...], hd ** -0.5)
    kh = prep(k_ref[...], kg_ref[...], 1.0)
    qh_ref[...] = qh.astype(jnp.bfloat16)
    kh_ref[...] = kh.astype(jnp.bfloat16)
    vb_ref[...] = v_ref[...].astype(jnp.bfloat16)

    km_ref[...] = jnp.zeros_like(km_ref)
    km_ref[0:nb, :] = jnp.sum(kh.reshape(nb, blk, hd), axis=1) * (1.0 / blk)
    gate = _dot_nt(qh, km_ref[...], _HI)

    lane = lax.broadcasted_iota(jnp.int32, (t, LANES), 1)
    qblk = lax.broadcasted_iota(jnp.int32, (t, LANES), 0) // blk
    past = lane < qblk
    gm = jnp.where(past, gate, _NEG_INF)
    rank = jnp.zeros((t, LANES), jnp.float32)
    for n2 in range(nb):
        coln = jnp.max(jnp.where(lane == n2, gm, _NEG_INF), axis=-1, keepdims=True)
        beats = (coln > gm) | ((coln == gm) & (lane > n2))
        rank = rank + beats.astype(jnp.float32)
    sel_ref[...] = (past & (rank < float(MOBA_TOPK))).astype(jnp.float32)

    lrow = lax.broadcasted_iota(jnp.int32, (blk, blk), 0)
    lcol = lax.broadcasted_iota(jnp.int32, (blk, blk), 1)
    causal = lrow >= lcol
    lane_b = lax.broadcasted_iota(jnp.int32, (blk, LANES), 1)

    for j in range(nb):
        rows = pl.ds(j * blk, blk)
        qj = qh_ref[rows, :]
        s = jnp.where(causal, _dot_nt(qj, kh_ref[rows, :]), _NEG_INF)
        m = jnp.max(s, axis=-1, keepdims=True)
        p = jnp.exp(s - m)
        l = jnp.sum(p, axis=-1, keepdims=True)
        acc = _dot(p.astype(jnp.bfloat16), vb_ref[rows, :])
        sel_j = sel_ref[rows, :]

        def body(n, carry):
            m, l, acc = carry
            krows = pl.ds(pl.multiple_of(n * blk, blk), blk)
            on = jnp.max(jnp.where(lane_b == n, sel_j, 0.0), axis=-1, keepdims=True) > 0.0
            s = jnp.where(on, _dot_nt(qj, kh_ref[krows, :]), _NEG_INF)
            m_new = jnp.maximum(m, jnp.max(s, axis=-1, keepdims=True))
            a = jnp.exp(m - m_new)
            p = jnp.exp(s - m_new)
            l = a * l + jnp.sum(p, axis=-1, keepdims=True)
            acc = a * acc + _dot(p.astype(jnp.bfloat16), vb_ref[krows, :])
            return m_new, l, acc

        if j > 0:
            m, l, acc = lax.fori_loop(0, j, body, (m, l, acc))
        o_ref[rows, :] = (acc / l).astype(o_ref.dtype)


def _moba(proj, q_g, k_g, cos_t, sa_t, sb_t, batch, t, col0):
    n = proj.shape[0]
    h, hd = MOBA_HEADS, MOBA_HD
    qb = col0 // hd
    kb = qb + h
    vb = kb + h
    tab = pl.BlockSpec((t, hd), lambda b, i: (0, 0))
    return pl.pallas_call(
        _moba_kernel,
        grid=(batch, h),
        in_specs=[
            pl.BlockSpec((t, hd), lambda b, i: (b, qb + i)),
            pl.BlockSpec((t, hd), lambda b, i: (b, kb + i)),
            pl.BlockSpec((t, hd), lambda b, i: (b, vb + i)),
            pl.BlockSpec((1, hd), lambda b, i: (0, 0)),
            pl.BlockSpec((1, hd), lambda b, i: (0, 0)),
            tab, tab, tab,
        ],
        out_specs=pl.BlockSpec((t, hd), lambda b, i: (b, i)),
        out_shape=jax.ShapeDtypeStruct((n, h * hd), jnp.bfloat16),
        scratch_shapes=[
            pltpu.VMEM((t, hd), jnp.bfloat16),
            pltpu.VMEM((t, hd), jnp.bfloat16),
            pltpu.VMEM((t, hd), jnp.bfloat16),
            pltpu.VMEM((LANES, hd), jnp.float32),
            pltpu.VMEM((t, LANES), jnp.float32),
        ],
        compiler_params=_params("parallel", "parallel"),
        name="moba",
    )(proj, proj, proj, q_g, k_g, cos_t, sa_t, sb_t)


def _out_proj_kernel(x_ref, a_ref, b_ref, wa_ref, wb_ref, o_ref):
    o_ref[...] = x_ref[...] + _dot(a_ref[...], wa_ref[...]) + _dot(b_ref[...], wb_ref[...])


def _out_proj(x2, o_gla, o_moba, w_a, w_b):
    n, d = x2.shape
    ka, kb = o_gla.shape[1], o_moba.shape[1]
    tm = _tile(n, 512)
    tn = _tile(d, 1024)
    return pl.pallas_call(
        _out_proj_kernel,
        grid=(n // tm, d // tn),
        in_specs=[
            pl.BlockSpec((tm, tn), lambda i, j: (i, j)),
            pl.BlockSpec((tm, ka), lambda i, j: (i, 0)),
            pl.BlockSpec((tm, kb), lambda i, j: (i, 0)),
            pl.BlockSpec((ka, tn), lambda i, j: (0, j)),
            pl.BlockSpec((kb, tn), lambda i, j: (0, j)),
        ],
        out_specs=pl.BlockSpec((tm, tn), lambda i, j: (i, j)),
        out_shape=jax.ShapeDtypeStruct((n, d), jnp.float32),
        compiler_params=_params("parallel", "parallel"),
        name="out_proj",
    )(x2, o_gla, o_moba, w_a, w_b)


def _peer_q_kernel(x_ref, g_ref, w_ref, keys_ref, xn_ref, st_ref, hb_ref):
    @pl.when(pl.program_id(1) == 0)
    def _():
        x = x_ref[...]
        r = lax.rsqrt(jnp.mean(x * x, axis=-1, keepdims=True) + EPS)
        hb = ((x * r) * g_ref[...]).astype(jnp.bfloat16)
        hb_ref[...] = hb
        xn_ref[...] = hb

    q = _dot(hb_ref[...], w_ref[...])
    half = q.shape[1] // 2
    for p in range(2):
        st_ref[0, p] = _dot_nt(keys_ref[0, p], q[:, p * half:(p + 1) * half], _HI)


def _peer_q(x1, g, w_pq, keys):
    n, d = x1.shape
    h, dq, nk = PEER_HEADS, PEER_DQ, PEER_NKEYS
    tm = _tile(n, 512)
    return pl.pallas_call(
        _peer_q_kernel,
        grid=(n // tm, h),
        in_specs=[
            pl.BlockSpec((tm, d), lambda i, j: (i, 0)),
            pl.BlockSpec((1, d), lambda i, j: (0, 0)),
            pl.BlockSpec((d, dq), lambda i, j: (0, j)),
            pl.BlockSpec((1, 2, nk, dq // 2), lambda i, j: (j, 0, 0, 0)),
        ],
        out_specs=[
            pl.BlockSpec((tm, d), lambda i, j: (i, 0)),
            pl.BlockSpec((1, 2, nk, tm), lambda i, j: (j, 0, 0, i)),
        ],
        out_shape=[
            jax.ShapeDtypeStruct((n, d), jnp.bfloat16),
            jax.ShapeDtypeStruct((h, 2, nk, n), jnp.float32),
        ],
        scratch_shapes=[pltpu.VMEM((tm, d), jnp.bfloat16)],
        compiler_params=_params("parallel", "arbitrary"),
        name="peer_q",
    )(x1, g, w_pq, keys)


def _extract_topk(cur, k, out_ref=None):
    rows = cur.shape[0]
    iota = lax.broadcasted_iota(jnp.int32, cur.shape, 0)
    mk = None
    for r in range(k):
        mk = jnp.max(cur, axis=0, keepdims=True)
        if out_ref is not None:
            out_ref[r:r + 1, :] = mk
        first = jnp.min(jnp.where(cur == mk, iota, rows), axis=0, keepdims=True)
        cur = jnp.where(iota == first, _NEG_INF, cur)
    return cur, mk


def _peer_topk_kernel(st_ref, s1m_ref, e1_ref, s2m_ref, e2n_ref, tau_ref, v1_ref, v2_ref):
    k = PEER_TOPK
    s1 = st_ref[0, 0]
    s2 = st_ref[0, 1]
    rem1, _ = _extract_topk(s1, k, v1_ref)
    rem2, _ = _extract_topk(s2, k, v2_ref)
    s1m_ref[0] = jnp.where(rem1 != s1, s1, _NEG_INF)
    s2m = jnp.where(rem2 != s2, s2, _NEG_INF)
    s2m_ref[0] = s2m
    v1 = v1_ref[...]
    v2 = v2_ref[...]
    m1 = v1[0:1, :]
    m2 = v2[0:1, :]
    e1_ref[0] = jnp.exp(s1 - m1)
    e2 = jnp.exp(s2 - m2)
    ev1 = jnp.exp(v1 - m1)
    ev2 = jnp.exp(v2 - m2)

    r8 = lax.broadcasted_iota(jnp.int32, (SUBLANES, v1.shape[1]), 0)
    r16 = lax.broadcasted_iota(jnp.int32, (2 * SUBLANES, v1.shape[1]), 0)

    def pieces(a, b, comb):
        lo_b, lo_a = b[0:SUBLANES, :], a[0:SUBLANES, :]
        return [
            (comb(a[0:1, :], b), None),
            (comb(a[1:2, :], lo_b), None),
            (comb(a[2:3, :], lo_b), r8 < 5),
            (comb(a[3:4, :], lo_b), r8 < 4),
            (comb(a, b[0:1, :]), r16 >= 4),
            (comb(lo_a, b[1:2, :]), r8 >= 4),
            (comb(lo_a, b[2:3, :]), r8 == 4),
        ]

    cand = jnp.concatenate(
        [c if msk is None else jnp.where(msk, c, _NEG_INF)
         for c, msk in pieces(v1, v2, lambda x, y: x + y)], axis=0)
    prod = jnp.concatenate([c for c, _ in pieces(ev1, ev2, lambda x, y: x * y)], axis=0)
    _, tau = _extract_topk(cand, k)
    zsum = jnp.sum(jnp.where(cand >= tau, prod, 0.0), axis=0, keepdims=True)
    e2n_ref[0] = e2 * (1.0 / zsum)
    tau_ref[0] = tau


def _peer_topk(st):
    h, _, nk, n = st.shape
    tl = _tile(n, 256)
    big = pl.BlockSpec((1, nk, tl), lambda i, j: (j, 0, i))
    shp = jax.ShapeDtypeStruct((h, nk, n), jnp.float32)
    return pl.pallas_call(
        _peer_topk_kernel,
        grid=(n // tl, h),
        in_specs=[pl.BlockSpec((1, 2, nk, tl), lambda i, j: (j, 0, 0, i))],
        out_specs=[big, big, big, big, pl.BlockSpec((1, 1, tl), lambda i, j: (j, 0, i))],
        out_shape=[shp, shp, shp, shp, jax.ShapeDtypeStruct((h, 1, n), jnp.float32)],
        scratch_shapes=[pltpu.VMEM((PEER_TOPK, tl), jnp.float32),
                        pltpu.VMEM((PEER_TOPK, tl), jnp.float32)],
        compiler_params=_params("parallel", "parallel"),
        name="peer_topk",
    )(st)


def _peer_ffn_kernel(x1_ref, xn_ref, u_ref, v_ref, s1m_ref, e1_ref, s2m_ref, e2n_ref, tau_ref, o_ref,
                     *, rows_per_step):
    e = pl.program_id(1)
    nh = s2m_ref.shape[0]
    nk = s2m_ref.shape[1]
    rblk = s1m_ref.shape[1]

    @pl.when(e == 0)
    def _():
        o_ref[...] = x1_ref[...]

    hmat = _dot_nt(xn_ref[...], u_ref[...])
    base = (e % (rblk // rows_per_step)) * rows_per_step
    slabs = []
    for r in range(rows_per_step):
        i8 = base + r
        acc = jnp.zeros((nk, xn_ref.shape[0]), jnp.float32)
        for h in range(nh):
            ssum = s1m_ref[h, pl.ds(i8, 1), :] + s2m_ref[h]
            w = e1_ref[h, pl.ds(i8, 1), :] * e2n_ref[h]
            acc = acc + jnp.where(ssum >= tau_ref[h], w, 0.0)
        slabs.append(acc.T)
    wsel = jnp.concatenate(slabs, axis=1) if len(slabs) > 1 else slabs[0]
    act = 0.5 * hmat * (1.0 + lax.erf(hmat * (1.0 / math.sqrt(2.0))))
    z = jnp.where(wsel != 0.0, wsel * act, 0.0).astype(jnp.bfloat16)
    o_ref[...] += _dot(z, v_ref[...])


def _peer_ffn(x1, xn, u_b, v_b, s1m, e1, s2m, e2n, tau):
    n, d = x1.shape
    ne = u_b.shape[0]
    nh, nk, _ = s1m.shape
    tm = _tile(n, 512)
    te = _tile(ne, 512)
    rows_per_step = te // nk
    rblk = max(SUBLANES, rows_per_step)
    steps_per_rblk = rblk // rows_per_step
    once = dict(pipeline_mode=pl.Buffered(1))
    return pl.pallas_call(
        functools.partial(_peer_ffn_kernel, rows_per_step=rows_per_step),
        grid=(n // tm, ne // te),
        in_specs=[
            pl.BlockSpec((tm, d), lambda i, j: (i, 0), **once),
            pl.BlockSpec((tm, d), lambda i, j: (i, 0), **once),
            pl.BlockSpec((te, d), lambda i, j: (j, 0)),
            pl.BlockSpec((te, d), lambda i, j: (j, 0)),
            pl.BlockSpec((nh, rblk, tm), lambda i, j: (0, j // steps_per_rblk, i)),
            pl.BlockSpec((nh, rblk, tm), lambda i, j: (0, j // steps_per_rblk, i)),
            pl.BlockSpec((nh, nk, tm), lambda i, j: (0, 0, i), **once),
            pl.BlockSpec((nh, nk, tm), lambda i, j: (0, 0, i), **once),
            pl.BlockSpec((nh, 1, tm), lambda i, j: (0, 0, i)),
        ],
        out_specs=pl.BlockSpec((tm, d), lambda i, j: (i, 0)),
        out_shape=jax.ShapeDtypeStruct((n, d), jnp.float32),
        compiler_params=_params("parallel", "arbitrary"),
        name="peer_ffn",
    )(x1, xn, u_b, v_b, s1m, e1, s2m, e2n, tau)


def _rope_tables(t, hd):
    half = ROPE_DIMS // 2
    inv = ROPE_THETA ** (-jnp.arange(half, dtype=jnp.float32) / half)
    ang = jnp.arange(t).astype(jnp.float32)[:, None] * inv[None, :]
    cos, sin = jnp.cos(ang), jnp.sin(ang)
    zeros = jnp.zeros((t, hd - 2 * half), jnp.float32)
    z16 = jnp.zeros((t, half), jnp.float32)
    cos_t = jnp.concatenate([cos, cos, jnp.ones_like(zeros)], axis=1)
    sa_t = jnp.concatenate([z16, sin, zeros], axis=1)
    sb_t = jnp.concatenate([-sin, z16, zeros], axis=1)
    return cos_t, sa_t, sb_t


def kernel(x, norm_mix_g, w_in, w_gate_up, b_gate, gla_norm_g, q_norm_g, k_norm_g, w_out, norm_ffn_g,
           peer_wq, peer_keys, peer_u, peer_v):
    batch, t, d = x.shape
    n = batch * t
    bf16 = jnp.bfloat16
    gla_qk_w = GLA_HEADS * GLA_DK
    gla_v_w = GLA_HEADS * GLA_DV
    moba_w = MOBA_HEADS * MOBA_HD
    lr0 = 2 * gla_qk_w + 2 * gla_v_w
    cos_t, sa_t, sb_t = _rope_tables(t, MOBA_HD)

    xcur = x.reshape(n, d)
    for l in range(w_in.shape[0]):
        wl = w_in[l]
        w_main = jnp.concatenate([wl[:, :lr0], wl[:, lr0 + GLA_GATE_RANK:]], axis=1).astype(bf16)
        w_lr = jnp.pad(wl[:, lr0:lr0 + GLA_GATE_RANK], ((0, 0), (0, LANES - GLA_GATE_RANK))).astype(bf16)
        wup_pad = jnp.pad(w_gate_up[l], ((0, LANES - GLA_GATE_RANK), (0, 0)))

        proj, glr = _in_proj(xcur, norm_mix_g[l][None, :], w_main, w_lr)
        o_gla = _gla(proj, glr, wup_pad, b_gate[l][None, :], gla_norm_g[l][None, :], batch, t)
        o_moba = _moba(proj, q_norm_g[l][None, :], k_norm_g[l][None, :], cos_t, sa_t, sb_t, batch, t, lr0)
        wo = w_out[l].astype(bf16)
        x1 = _out_proj(xcur, o_gla, o_moba, wo[:gla_v_w], wo[gla_v_w:gla_v_w + moba_w])

        xn, st = _peer_q(x1, norm_ffn_g[l][None, :], peer_wq[l].astype(bf16), peer_keys[l])
        s1m, e1, s2m, e2n, tau = _peer_topk(st)
        xcur = _peer_ffn(x1, xn, peer_u[l].astype(bf16), peer_v[l].astype(bf16), s1m, e1, s2m, e2n, tau)
    return xcur.reshape(batch, t, d)
```

```python
import functools
import math

import jax
import jax.numpy as jnp
from jax import lax
from jax.experimental import pallas as pl
from jax.experimental.pallas import tpu as pltpu

GLA_HEADS = 8
GLA_DK = 128
GLA_DV = 256
GLA_GATE_RANK = 16
GLA_GATE_NORMALIZER = 16.0
GLA_CHUNK = 64
MOBA_HEADS = 16
MOBA_HD = 128
MOBA_BLOCK = 256
MOBA_TOPK = 3
ROPE_THETA = 500000.0
ROPE_DIMS = MOBA_HD // 4
PEER_HEADS = 8
PEER_NKEYS = 128
PEER_DQ = 256
PEER_TOPK = 16
EPS = 1e-6

LANES = 128
SUBLANES = 8
VMEM_LIMIT_BYTES = 60 * 1024 * 1024

_HI = lax.Precision.HIGHEST
_NEG_INF = float("-inf")


def _dot(a, b, precision=None):
    return lax.dot_general(a, b, (((1,), (0,)), ((), ())), precision=precision,
                           preferred_element_type=jnp.float32)


def _dot_nt(a, b, precision=None):
    return lax.dot_general(a, b, (((1,), (1,)), ((), ())), precision=precision,
                           preferred_element_type=jnp.float32)


def _dot_tn(a, b, precision=None):
    return lax.dot_general(a, b, (((0,), (0,)), ((), ())), precision=precision,
                           preferred_element_type=jnp.float32)


def _tile(n, pref):
    t = min(n, pref)
    while n % t:
        t //= 2
    return t


def _params(*sem):
    return pltpu.CompilerParams(dimension_semantics=sem, vmem_limit_bytes=VMEM_LIMIT_BYTES)


def _in_proj_kernel(x_ref, g_ref, w_ref, wlr_ref, o_ref, olr_ref, hb_ref):
    @pl.when(pl.program_id(1) == 0)
    def _():
        x = x_ref[...]
        r = lax.rsqrt(jnp.mean(x * x, axis=-1, keepdims=True) + EPS)
        hb = ((x * r) * g_ref[...]).astype(jnp.bfloat16)
        hb_ref[...] = hb
        olr_ref[...] = _dot(hb, wlr_ref[...])

    o_ref[...] = _dot(hb_ref[...], w_ref[...])


def _in_proj(x2, g, w_main, w_lr):
    n, d = x2.shape
    cols = w_main.shape[1]
    tm = _tile(n, 512)
    tn = _tile(cols, 1024)
    return pl.pallas_call(
        _in_proj_kernel,
        grid=(n // tm, cols // tn),
        in_specs=[
            pl.BlockSpec((tm, d), lambda i, j: (i, 0)),
            pl.BlockSpec((1, d), lambda i, j: (0, 0)),
            pl.BlockSpec((d, tn), lambda i, j: (0, j)),
            pl.BlockSpec((d, LANES), lambda i, j: (0, 0)),
        ],
        out_specs=[
            pl.BlockSpec((tm, tn), lambda i, j: (i, j)),
            pl.BlockSpec((tm, LANES), lambda i, j: (i, 0)),
        ],
        out_shape=[
            jax.ShapeDtypeStruct((n, cols), jnp.float32),
            jax.ShapeDtypeStruct((n, LANES), jnp.float32),
        ],
        scratch_shapes=[pltpu.VMEM((tm, d), jnp.bfloat16)],
        compiler_params=_params("parallel", "arbitrary"),
        name="in_proj",
    )(x2, g, w_main, w_lr)


GLA_GROUP = 256


def _gla_kernel(q_ref, k_ref, v_ref, go_ref, glr_ref, wup_ref, bg_ref, ng_ref, o_ref,
                qd_ref, ki_ref, ks_ref, vb_ref, oacc_ref, dec_ref, upd_ref, sprev_ref):
    t, dk = q_ref.shape
    dv = v_ref.shape[1]
    c = GLA_CHUNK
    n = t // c
    grp = min(GLA_GROUP, t)
    scale = dk ** -0.5
    bf16 = jnp.bfloat16

    z = _dot(glr_ref[...], wup_ref[...], _HI) + bg_ref[...]
    log_a = (jnp.minimum(z, 0.0) - jnp.log1p(jnp.exp(-jnp.abs(z)))) / GLA_GATE_NORMALIZER
    row = lax.broadcasted_iota(jnp.int32, (grp, grp), 0)
    col = lax.broadcasted_iota(jnp.int32, (grp, grp), 1)
    same_chunk_causal = (row >= col) & ((row // c) == (col // c))
    tril = same_chunk_causal.astype(jnp.float32)
    bc = jnp.concatenate([_dot(tril, log_a[g * grp:(g + 1) * grp, :], _HI) for g in range(t // grp)], axis=0)
    bc3 = bc.reshape(n, c, dk)
    bl3 = bc3[:, c - 1:c, :]
    k3 = k_ref[...].reshape(n, c, dk)
    qd_ref[...] = (q_ref[...] * scale * jnp.exp(bc)).astype(bf16)
    ki_ref[...] = (k_ref[...] * jnp.exp(-bc)).astype(bf16)
    ks_ref[...] = (k3 * jnp.exp(bl3 - bc3)).reshape(t, dk).astype(bf16)
    dec_ref[...] = jnp.exp(bl3)
    vb_ref[...] = v_ref[...].astype(bf16)

    for g in range(t // grp):
        rows = pl.ds(g * grp, grp)
        att = jnp.where(same_chunk_causal, _dot_nt(qd_ref[rows, :], ki_ref[rows, :]), 0.0)
        oacc_ref[rows, :] = _dot(att.astype(bf16), vb_ref[rows, :])

    def upd_body(ci, carry):
        rows = pl.ds(pl.multiple_of(ci * c, c), c)
        upd_ref[ci] = _dot_tn(vb_ref[rows, :], ks_ref[rows, :])
        return carry

    lax.fori_loop(0, n, upd_body, 0, unroll=4)

    def scan_body(ci, st):
        sprev_ref[ci] = st.astype(bf16)
        return dec_ref[ci] * st + upd_ref[ci]

    lax.fori_loop(0, n, scan_body, jnp.zeros((dv, dk), jnp.float32), unroll=2)

    def inter_body(ci, carry):
        rows = pl.ds(pl.multiple_of(ci * c, c), c)
        oacc_ref[rows, :] += _dot_nt(qd_ref[rows, :], sprev_ref[ci])
        return carry

    lax.fori_loop(0, n, inter_body, 0, unroll=4)

    o = oacc_ref[...]
    r = lax.rsqrt(jnp.mean(o * o, axis=-1, keepdims=True) + EPS)
    go = go_ref[...]
    o_ref[...] = (((o * r) * ng_ref[...]) * (go * jax.nn.sigmoid(go))).astype(o_ref.dtype)


def _gla(proj, glr, wup_pad, b_gate, norm_g, batch, t):
    n = proj.shape[0]
    h, dk, dv = GLA_HEADS, GLA_DK, GLA_DV
    nc = t // GLA_CHUNK
    qk_w = h * dk
    kb = qk_w // dk
    vb = (2 * qk_w) // dv
    gb = (2 * qk_w + h * dv) // dv
    return pl.pallas_call(
        _gla_kernel,
        grid=(batch, h),
        in_specs=[
            pl.BlockSpec((t, dk), lambda b, i: (b, i)),
            pl.BlockSpec((t, dk), lambda b, i: (b, kb + i)),
            pl.BlockSpec((t, dv), lambda b, i: (b, vb + i)),
            pl.BlockSpec((t, dv), lambda b, i: (b, gb + i)),
            pl.BlockSpec((t, LANES), lambda b, i: (b, 0)),
            pl.BlockSpec((LANES, dk), lambda b, i: (0, i)),
            pl.BlockSpec((1, dk), lambda b, i: (0, i)),
            pl.BlockSpec((1, dv), lambda b, i: (0, 0)),
        ],
        out_specs=pl.BlockSpec((t, dv), lambda b, i: (b, i)),
        out_shape=jax.ShapeDtypeStruct((n, h * dv), jnp.bfloat16),
        scratch_shapes=[
            pltpu.VMEM((t, dk), jnp.bfloat16),
            pltpu.VMEM((t, dk), jnp.bfloat16),
            pltpu.VMEM((t, dk), jnp.bfloat16),
            pltpu.VMEM((t, dv), jnp.bfloat16),
            pltpu.VMEM((t, dv), jnp.float32),
            pltpu.VMEM((nc, 1, dk), jnp.float32),
            pltpu.VMEM((nc, dv, dk), jnp.float32),
            pltpu.VMEM((nc, dv, dk), jnp.bfloat16),
        ],
        compiler_params=_params("parallel", "parallel"),
        name="gla",
    )(proj, proj, proj, proj, glr, wup_pad, b_gate, norm_g)


MASK_VALUE = -1e30


def _moba_kernel(q_ref, k_ref, v_ref, qg_ref, kg_ref, cos_ref, sa_ref, sb_ref, o_ref,
                 qa_ref, ka_ref, vb_ref, bias_ref):
    t, hd = q_ref.shape
    blk = MOBA_BLOCK
    nb = t // blk
    half = ROPE_DIMS // 2
    bf16 = jnp.bfloat16

    def prep(x, g, scale):
        r = lax.rsqrt(jnp.mean(x * x, axis=-1, keepdims=True) + EPS)
        xn = (x * r) * g
        y = (xn * cos_ref[...] + pltpu.roll(xn, half, 1) * sa_ref[...]
             + pltpu.roll(xn, hd - half, 1) * sb_ref[...])
        return y * scale

    qh = prep(q_ref[...], qg_ref[...], hd ** -0.5)
    kh = prep(k_ref[...], kg_ref[...], 1.0)
    vb_ref[...] = v_ref[...].astype(bf16)

    km = jnp.sum(kh.reshape(nb, blk, hd), axis=1) * (1.0 / blk)
    gate_t = _dot_nt(km, qh, _HI)
    kblk = lax.broadcasted_iota(jnp.int32, (nb, t), 0)
    qblk = lax.broadcasted_iota(jnp.int32, (nb, t), 1) // blk
    past = kblk < qblk
    gm = jnp.where(past, gate_t, _NEG_INF)
    rank = jnp.zeros((nb, t), jnp.float32)
    for n2 in range(nb):
        other = gm[n2:n2 + 1, :]
        beats = (other > gm) | ((other == gm) & (kblk > n2))
        rank = rank + beats.astype(jnp.float32)
    allowed = (past & (rank < float(MOBA_TOPK))) | (kblk == qblk)
    bias_ref[...] = jnp.zeros_like(bias_ref)
    bias_ref[0:nb, :] = jnp.where(allowed, 0.0, MASK_VALUE)

    qa_ref[:, 0:hd] = qh.astype(bf16)
    qa_ref[:, hd:2 * hd] = bias_ref[...].T.astype(bf16)
    ka_ref[:, 0:hd] = kh.astype(bf16)
    lane = lax.broadcasted_iota(jnp.int32, (t, hd), 1)
    rblk = lax.broadcasted_iota(jnp.int32, (t, hd), 0) // blk
    ka_ref[:, hd:2 * hd] = (lane == rblk).astype(bf16)

    lrow = lax.broadcasted_iota(jnp.int32, (blk, blk), 0)
    lcol = lax.broadcasted_iota(jnp.int32, (blk, blk), 1)
    causal = lrow >= lcol
    for j in range(nb):
        rows = pl.ds(j * blk, blk)
        nkeys = (j + 1) * blk
        s = _dot_nt(qa_ref[rows, :], ka_ref[0:nkeys, :])
        own = jnp.where(causal, s[:, j * blk:nkeys], MASK_VALUE)
        s = own if j == 0 else jnp.concatenate([s[:, :j * blk], own], axis=1)
        m = jnp.max(s, axis=-1, keepdims=True)
        p = jnp.exp(s - m)
        l = jnp.sum(p, axis=-1, keepdims=True)
        o = _dot(p.astype(bf16), vb_ref[0:nkeys, :])
        o_ref[rows, :] = (o * (1.0 / l)).astype(o_ref.dtype)


def _moba(proj, q_g, k_g, cos_t, sa_t, sb_t, batch, t, col0):
    n = proj.shape[0]
    h, hd = MOBA_HEADS, MOBA_HD
    qb = col0 // hd
    kb = qb + h
    vb = kb + h
    tab = pl.BlockSpec((t, hd), lambda b, i: (0, 0))
    return pl.pallas_call(
        _moba_kernel,
        grid=(batch, h),
        in_specs=[
            pl.BlockSpec((t, hd), lambda b, i: (b, qb + i)),
            pl.BlockSpec((t, hd), lambda b, i: (b, kb + i)),
            pl.BlockSpec((t, hd), lambda b, i: (b, vb + i)),
            pl.BlockSpec((1, hd), lambda b, i: (0, 0)),
            pl.BlockSpec((1, hd), lambda b, i: (0, 0)),
            tab, tab, tab,
        ],
        out_specs=pl.BlockSpec((t, hd), lambda b, i: (b, i)),
        out_shape=jax.ShapeDtypeStruct((n, h * hd), jnp.bfloat16),
        scratch_shapes=[
            pltpu.VMEM((t, 2 * hd), jnp.bfloat16),
            pltpu.VMEM((t, 2 * hd), jnp.bfloat16),
            pltpu.VMEM((t, hd), jnp.bfloat16),
            pltpu.VMEM((LANES, t), jnp.float32),
        ],
        compiler_params=_params("parallel", "parallel"),
        name="moba",
    )(proj, proj, proj, q_g, k_g, cos_t, sa_t, sb_t)


def _out_proj_kernel(x_ref, a_ref, b_ref, wa_ref, wb_ref, o_ref):
    o_ref[...] = x_ref[...] + _dot(a_ref[...], wa_ref[...]) + _dot(b_ref[...], wb_ref[...])


def _out_proj(x2, o_gla, o_moba, w_a, w_b):
    n, d = x2.shape
    ka, kb = o_gla.shape[1], o_moba.shape[1]
    tm = _tile(n, 512)
    tn = _tile(d, 1024)
    return pl.pallas_call(
        _out_proj_kernel,
        grid=(n // tm, d // tn),
        in_specs=[
            pl.BlockSpec((tm, tn), lambda i, j: (i, j)),
            pl.BlockSpec((tm, ka), lambda i, j: (i, 0)),
            pl.BlockSpec((tm, kb), lambda i, j: (i, 0)),
            pl.BlockSpec((ka, tn), lambda i, j: (0, j)),
            pl.BlockSpec((kb, tn), lambda i, j: (0, j)),
        ],
        out_specs=pl.BlockSpec((tm, tn), lambda i, j: (i, j)),
        out_shape=jax.ShapeDtypeStruct((n, d), jnp.float32),
        compiler_params=_params("parallel", "parallel"),
        name="out_proj",
    )(x2, o_gla, o_moba, w_a, w_b)


def _peer_q_kernel(x_ref, g_ref, w_ref, keys_ref, xn_ref, st_ref, hb_ref):
    @pl.when(pl.program_id(1) == 0)
    def _():
        x = x_ref[...]
        r = lax.rsqrt(jnp.mean(x * x, axis=-1, keepdims=True) + EPS)
        hb = ((x * r) * g_ref[...]).astype(jnp.bfloat16)
        hb_ref[...] = hb
        xn_ref[...] = hb

    q = _dot(hb_ref[...], w_ref[...])
    half = q.shape[1] // 2
    for p in range(2):
        st_ref[0, p] = _dot_nt(keys_ref[0, p], q[:, p * half:(p + 1) * half], _HI)


def _peer_q(x1, g, w_pq, keys):
    n, d = x1.shape
    h, dq, nk = PEER_HEADS, PEER_DQ, PEER_NKEYS
    tm = _tile(n, 512)
    return pl.pallas_call(
        _peer_q_kernel,
        grid=(n // tm, h),
        in_specs=[
            pl.BlockSpec((tm, d), lambda i, j: (i, 0)),
            pl.BlockSpec((1, d), lambda i, j: (0, 0)),
            pl.BlockSpec((d, dq), lambda i, j: (0, j)),
            pl.BlockSpec((1, 2, nk, dq // 2), lambda i, j: (j, 0, 0, 0)),
        ],
        out_specs=[
            pl.BlockSpec((tm, d), lambda i, j: (i, 0)),
            pl.BlockSpec((1, 2, nk, tm), lambda i, j: (j, 0, 0, i)),
        ],
        out_shape=[
            jax.ShapeDtypeStruct((n, d), jnp.bfloat16),
            jax.ShapeDtypeStruct((h, 2, nk, n), jnp.float32),
        ],
        scratch_shapes=[pltpu.VMEM((tm, d), jnp.bfloat16)],
        compiler_params=_params("parallel", "arbitrary"),
        name="peer_q",
    )(x1, g, w_pq, keys)


def _extract_topk(cur, k, out_ref=None):
    rows = cur.shape[0]
    iota = lax.broadcasted_iota(jnp.int32, cur.shape, 0)
    mk = None
    for r in range(k):
        mk = jnp.max(cur, axis=0, keepdims=True)
        if out_ref is not None:
            out_ref[r:r + 1, :] = mk
        first = jnp.min(jnp.where(cur == mk, iota, rows), axis=0, keepdims=True)
        cur = jnp.where(iota == first, _NEG_INF, cur)
    return cur, mk


def _peer_topk_kernel(st_ref, s1m_ref, e1_ref, s2m_ref, e2n_ref, tau_ref, v1_ref, v2_ref):
    k = PEER_TOPK
    s1 = st_ref[0, 0]
    s2 = st_ref[0, 1]
    rem1, _ = _extract_topk(s1, k, v1_ref)
    rem2, _ = _extract_topk(s2, k, v2_ref)
    s1m_ref[0] = jnp.where(rem1 != s1, s1, _NEG_INF)
    s2m = jnp.where(rem2 != s2, s2, _NEG_INF)
    s2m_ref[0] = s2m
    v1 = v1_ref[...]
    v2 = v2_ref[...]
    m1 = v1[0:1, :]
    m2 = v2[0:1, :]
    e1_ref[0] = jnp.exp(s1 - m1)
    e2 = jnp.exp(s2 - m2)
    ev1 = jnp.exp(v1 - m1)
    ev2 = jnp.exp(v2 - m2)

    r8 = lax.broadcasted_iota(jnp.int32, (SUBLANES, v1.shape[1]), 0)
    r16 = lax.broadcasted_iota(jnp.int32, (2 * SUBLANES, v1.shape[1]), 0)

    def pieces(a, b, comb):
        lo_b, lo_a = b[0:SUBLANES, :], a[0:SUBLANES, :]
        return [
            (comb(a[0:1, :], b), None),
            (comb(a[1:2, :], lo_b), None),
            (comb(a[2:3, :], lo_b), r8 < 5),
            (comb(a[3:4, :], lo_b), r8 < 4),
            (comb(a, b[0:1, :]), r16 >= 4),
            (comb(lo_a, b[1:2, :]), r8 >= 4),
            (comb(lo_a, b[2:3, :]), r8 == 4),
        ]

    cand = jnp.concatenate(
        [c if msk is None else jnp.where(msk, c, _NEG_INF)
         for c, msk in pieces(v1, v2, lambda x, y: x + y)], axis=0)
    prod = jnp.concatenate([c for c, _ in pieces(ev1, ev2, lambda x, y: x * y)], axis=0)
    _, tau = _extract_topk(cand, k)
    zsum = jnp.sum(jnp.where(cand >= tau, prod, 0.0), axis=0, keepdims=True)
    e2n_ref[0] = e2 * (1.0 / zsum)
    tau_ref[0] = tau


def _peer_topk(st):
    h, _, nk, n = st.shape
    tl = _tile(n, 256)
    big = pl.BlockSpec((1, nk, tl), lambda i, j: (j, 0, i))
    shp = jax.ShapeDtypeStruct((h, nk, n), jnp.float32)
    return pl.pallas_call(
        _peer_topk_kernel,
        grid=(n // tl, h),
        in_specs=[pl.BlockSpec((1, 2, nk, tl), lambda i, j: (j, 0, 0, i))],
        out_specs=[big, big, big, big, pl.BlockSpec((1, 1, tl), lambda i, j: (j, 0, i))],
        out_shape=[shp, shp, shp, shp, jax.ShapeDtypeStruct((h, 1, n), jnp.float32)],
        scratch_shapes=[pltpu.VMEM((PEER_TOPK, tl), jnp.float32),
                        pltpu.VMEM((PEER_TOPK, tl), jnp.float32)],
        compiler_params=_params("parallel", "parallel"),
        name="peer_topk",
    )(st)


def _peer_ffn_kernel(x1_ref, xn_ref, u_ref, v_ref, s1m_ref, e1_ref, s2m_ref, e2n_ref, tau_ref, o_ref,
                     *, rows_per_step):
    e = pl.program_id(1)
    nh = s2m_ref.shape[0]
    nk = s2m_ref.shape[1]
    rblk = s1m_ref.shape[1]

    @pl.when(e == 0)
    def _():
        o_ref[...] = x1_ref[...]

    hmat = _dot_nt(xn_ref[...], u_ref[...])
    base = (e % (rblk // rows_per_step)) * rows_per_step
    slabs = []
    for r in range(rows_per_step):
        i8 = base + r
        acc = jnp.zeros((nk, xn_ref.shape[0]), jnp.float32)
        for h in range(nh):
            ssum = s1m_ref[h, pl.ds(i8, 1), :] + s2m_ref[h]
            w = e1_ref[h, pl.ds(i8, 1), :] * e2n_ref[h]
            acc = acc + jnp.where(ssum >= tau_ref[h], w, 0.0)
        slabs.append(acc.T)
    wsel = jnp.concatenate(slabs, axis=1) if len(slabs) > 1 else slabs[0]
    act = 0.5 * hmat * (1.0 + lax.erf(hmat * (1.0 / math.sqrt(2.0))))
    z = jnp.where(wsel != 0.0, wsel * act, 0.0).astype(jnp.bfloat16)
    o_ref[...] += _dot(z, v_ref[...])


def _peer_ffn(x1, xn, u_b, v_b, s1m, e1, s2m, e2n, tau):
    n, d = x1.shape
    ne = u_b.shape[0]
    nh, nk, _ = s1m.shape
    tm = _tile(n, 512)
    te = _tile(ne, 512)
    rows_per_step = te // nk
    rblk = max(SUBLANES, rows_per_step)
    steps_per_rblk = rblk // rows_per_step
    once = dict(pipeline_mode=pl.Buffered(1))
    return pl.pallas_call(
        functools.partial(_peer_ffn_kernel, rows_per_step=rows_per_step),
        grid=(n // tm, ne // te),
        in_specs=[
            pl.BlockSpec((tm, d), lambda i, j: (i, 0), **once),
            pl.BlockSpec((tm, d), lambda i, j: (i, 0), **once),
            pl.BlockSpec((te, d), lambda i, j: (j, 0)),
            pl.BlockSpec((te, d), lambda i, j: (j, 0)),
            pl.BlockSpec((nh, rblk, tm), lambda i, j: (0, j // steps_per_rblk, i)),
            pl.BlockSpec((nh, rblk, tm), lambda i, j: (0, j // steps_per_rblk, i)),
            pl.BlockSpec((nh, nk, tm), lambda i, j: (0, 0, i), **once),
            pl.BlockSpec((nh, nk, tm), lambda i, j: (0, 0, i), **once),
            pl.BlockSpec((nh, 1, tm), lambda i, j: (0, 0, i)),
        ],
        out_specs=pl.BlockSpec((tm, d), lambda i, j: (i, 0)),
        out_shape=jax.ShapeDtypeStruct((n, d), jnp.float32),
        compiler_params=_params("parallel", "arbitrary"),
        name="peer_ffn",
    )(x1, xn, u_b, v_b, s1m, e1, s2m, e2n, tau)


def _rope_tables(t, hd):
    half = ROPE_DIMS // 2
    inv = ROPE_THETA ** (-jnp.arange(half, dtype=jnp.float32) / half)
    ang = jnp.arange(t).astype(jnp.float32)[:, None] * inv[None, :]
    cos, sin = jnp.cos(ang), jnp.sin(ang)
    zeros = jnp.zeros((t, hd - 2 * half), jnp.float32)
    z16 = jnp.zeros((t, half), jnp.float32)
    cos_t = jnp.concatenate([cos, cos, jnp.ones_like(zeros)], axis=1)
    sa_t = jnp.concatenate([z16, sin, zeros], axis=1)
    sb_t = jnp.concatenate([-sin, z16, zeros], axis=1)
    return cos_t, sa_t, sb_t


def kernel(x, norm_mix_g, w_in, w_gate_up, b_gate, gla_norm_g, q_norm_g, k_norm_g, w_out, norm_ffn_g,
           peer_wq, peer_keys, peer_u, peer_v):
    batch, t, d = x.shape
    n = batch * t
    bf16 = jnp.bfloat16
    gla_qk_w = GLA_HEADS * GLA_DK
    gla_v_w = GLA_HEADS * GLA_DV
    moba_w = MOBA_HEADS * MOBA_HD
    lr0 = 2 * gla_qk_w + 2 * gla_v_w
    cos_t, sa_t, sb_t = _rope_tables(t, MOBA_HD)

    xcur = x.reshape(n, d)
    for l in range(w_in.shape[0]):
        wl = w_in[l]
        w_main = jnp.concatenate([wl[:, :lr0], wl[:, lr0 + GLA_GATE_RANK:]], axis=1).astype(bf16)
        w_lr = jnp.pad(wl[:, lr0:lr0 + GLA_GATE_RANK], ((0, 0), (0, LANES - GLA_GATE_RANK))).astype(bf16)
        wup_pad = jnp.pad(w_gate_up[l], ((0, LANES - GLA_GATE_RANK), (0, 0)))

        proj, glr = _in_proj(xcur, norm_mix_g[l][None, :], w_main, w_lr)
        o_gla = _gla(proj, glr, wup_pad, b_gate[l][None, :], gla_norm_g[l][None, :], batch, t)
        o_moba = _moba(proj, q_norm_g[l][None, :], k_norm_g[l][None, :], cos_t, sa_t, sb_t, batch, t, lr0)
        wo = w_out[l].astype(bf16)
        x1 = _out_proj(xcur, o_gla, o_moba, wo[:gla_v_w], wo[gla_v_w:gla_v_w + moba_w])

        xn, st = _peer_q(x1, norm_ffn_g[l][None, :], peer_wq[l].astype(bf16), peer_keys[l])
        s1m, e1, s2m, e2n, tau = _peer_topk(st)
        xcur = _peer_ffn(x1, xn, peer_u[l].astype(bf16), peer_v[l].astype(bf16), s1m, e1, s2m, e2n, tau)
    return xcur.reshape(batch, t, d)
```

```python
import functools
import math

import jax
import jax.numpy as jnp
from jax import lax
from jax.experimental import pallas as pl
from jax.experimental.pallas import tpu as pltpu

GLA_HEADS = 8
GLA_DK = 128
GLA_DV = 256
GLA_GATE_RANK = 16
GLA_GATE_NORMALIZER = 16.0
GLA_CHUNK = 64
MOBA_HEADS = 16
MOBA_HD = 128
MOBA_BLOCK = 256
MOBA_TOPK = 3
ROPE_THETA = 500000.0
ROPE_DIMS = MOBA_HD // 4
PEER_HEADS = 8
PEER_NKEYS = 128
PEER_DQ = 256
PEER_TOPK = 16
EPS = 1e-6

LANES = 128
SUBLANES = 8
VMEM_LIMIT_BYTES = 60 * 1024 * 1024

_HI = lax.Precision.HIGHEST
_NEG_INF = float("-inf")


def _dot(a, b, precision=None):
    return lax.dot_general(a, b, (((1,), (0,)), ((), ())), precision=precision,
                           preferred_element_type=jnp.float32)


def _dot_nt(a, b, precision=None):
    return lax.dot_general(a, b, (((1,), (1,)), ((), ())), precision=precision,
                           preferred_element_type=jnp.float32)


def _dot_tn(a, b, precision=None):
    return lax.dot_general(a, b, (((0,), (0,)), ((), ())), precision=precision,
                           preferred_element_type=jnp.float32)


def _tile(n, pref):
    t = min(n, pref)
    while n % t:
        t //= 2
    return t


def _params(*sem, flags=None):
    return pltpu.CompilerParams(dimension_semantics=sem, vmem_limit_bytes=VMEM_LIMIT_BYTES, flags=flags)


def _in_proj_kernel(x_ref, g_ref, w_ref, wlr_ref, o_ref, olr_ref, hb_ref):
    @pl.when(pl.program_id(1) == 0)
    def _():
        x = x_ref[...]
        r = lax.rsqrt(jnp.mean(x * x, axis=-1, keepdims=True) + EPS)
        hb = ((x * r) * g_ref[...]).astype(jnp.bfloat16)
        hb_ref[...] = hb
        olr_ref[...] = _dot(hb, wlr_ref[...])

    o_ref[...] = _dot(hb_ref[...], w_ref[...])


def _in_proj(x2, g, w_main, w_lr):
    n, d = x2.shape
    cols = w_main.shape[1]
    tm = _tile(n, 512)
    tn = _tile(cols, 1024)
    return pl.pallas_call(
        _in_proj_kernel,
        grid=(n // tm, cols // tn),
        in_specs=[
            pl.BlockSpec((tm, d), lambda i, j: (i, 0)),
            pl.BlockSpec((1, d), lambda i, j: (0, 0)),
            pl.BlockSpec((d, tn), lambda i, j: (0, j)),
            pl.BlockSpec((d, LANES), lambda i, j: (0, 0)),
        ],
        out_specs=[
            pl.BlockSpec((tm, tn), lambda i, j: (i, j)),
            pl.BlockSpec((tm, LANES), lambda i, j: (i, 0)),
        ],
        out_shape=[
            jax.ShapeDtypeStruct((n, cols), jnp.float32),
            jax.ShapeDtypeStruct((n, LANES), jnp.float32),
        ],
        scratch_shapes=[pltpu.VMEM((tm, d), jnp.bfloat16)],
        compiler_params=_params("parallel", "arbitrary"),
        name="in_proj",
    )(x2, g, w_main, w_lr)


GLA_GROUP = 256


def _gla_kernel(q_ref, k_ref, v_ref, go_ref, glr_ref, wup_ref, bg_ref, ng_ref, o_ref,
                qd_ref, ki_ref, ks_ref, vb_ref, oacc_ref, dec_ref, upd_ref, sprev_ref):
    t, dk = q_ref.shape
    dv = v_ref.shape[1]
    c = GLA_CHUNK
    n = t // c
    grp = min(GLA_GROUP, t)
    scale = dk ** -0.5
    bf16 = jnp.bfloat16

    z = _dot(glr_ref[...], wup_ref[...], _HI) + bg_ref[...]
    log_a = (jnp.minimum(z, 0.0) - jnp.log1p(jnp.exp(-jnp.abs(z)))) / GLA_GATE_NORMALIZER
    row = lax.broadcasted_iota(jnp.int32, (grp, grp), 0)
    col = lax.broadcasted_iota(jnp.int32, (grp, grp), 1)
    same_chunk_causal = (row >= col) & ((row // c) == (col // c))
    tril = same_chunk_causal.astype(jnp.float32)
    bc = jnp.concatenate([_dot(tril, log_a[g * grp:(g + 1) * grp, :], _HI) for g in range(t // grp)], axis=0)
    bc3 = bc.reshape(n, c, dk)
    bl3 = bc3[:, c - 1:c, :]
    k3 = k_ref[...].reshape(n, c, dk)
    qd_ref[...] = (q_ref[...] * scale * jnp.exp(bc)).astype(bf16)
    ki_ref[...] = (k_ref[...] * jnp.exp(-bc)).astype(bf16)
    ks_ref[...] = (k3 * jnp.exp(bl3 - bc3)).reshape(t, dk).astype(bf16)
    dec_ref[...] = jnp.exp(bl3)
    vb_ref[...] = v_ref[...].astype(bf16)

    for g in range(t // grp):
        rows = pl.ds(g * grp, grp)
        att = jnp.where(same_chunk_causal, _dot_nt(qd_ref[rows, :], ki_ref[rows, :]), 0.0)
        oacc_ref[rows, :] = _dot(att.astype(bf16), vb_ref[rows, :])

    def upd_body(ci, carry):
        rows = pl.ds(pl.multiple_of(ci * c, c), c)
        upd_ref[ci] = _dot_tn(vb_ref[rows, :], ks_ref[rows, :])
        return carry

    lax.fori_loop(0, n, upd_body, 0, unroll=4)

    def scan_body(ci, st):
        sprev_ref[ci] = st.astype(bf16)
        return dec_ref[ci] * st + upd_ref[ci]

    lax.fori_loop(0, n, scan_body, jnp.zeros((dv, dk), jnp.float32), unroll=2)

    def inter_body(ci, carry):
        rows = pl.ds(pl.multiple_of(ci * c, c), c)
        oacc_ref[rows, :] += _dot_nt(qd_ref[rows, :], sprev_ref[ci])
        return carry

    lax.fori_loop(0, n, inter_body, 0, unroll=4)

    o = oacc_ref[...]
    r = lax.rsqrt(jnp.mean(o * o, axis=-1, keepdims=True) + EPS)
    go = go_ref[...]
    o_ref[...] = (((o * r) * ng_ref[...]) * (go * jax.nn.sigmoid(go))).astype(o_ref.dtype)


def _gla(proj, glr, wup_pad, b_gate, norm_g, batch, t):
    n = proj.shape[0]
    h, dk, dv = GLA_HEADS, GLA_DK, GLA_DV
    nc = t // GLA_CHUNK
    qk_w = h * dk
    kb = qk_w // dk
    vb = (2 * qk_w) // dv
    gb = (2 * qk_w + h * dv) // dv
    return pl.pallas_call(
        _gla_kernel,
        grid=(batch, h),
        in_specs=[
            pl.BlockSpec((t, dk), lambda b, i: (b, i)),
            pl.BlockSpec((t, dk), lambda b, i: (b, kb + i)),
            pl.BlockSpec((t, dv), lambda b, i: (b, vb + i)),
            pl.BlockSpec((t, dv), lambda b, i: (b, gb + i)),
            pl.BlockSpec((t, LANES), lambda b, i: (b, 0)),
            pl.BlockSpec((LANES, dk), lambda b, i: (0, i)),
            pl.BlockSpec((1, dk), lambda b, i: (0, i)),
            pl.BlockSpec((1, dv), lambda b, i: (0, 0)),
        ],
        out_specs=pl.BlockSpec((t, dv), lambda b, i: (b, i)),
        out_shape=jax.ShapeDtypeStruct((n, h * dv), jnp.bfloat16),
        scratch_shapes=[
            pltpu.VMEM((t, dk), jnp.bfloat16),
            pltpu.VMEM((t, dk), jnp.bfloat16),
            pltpu.VMEM((t, dk), jnp.bfloat16),
            pltpu.VMEM((t, dv), jnp.bfloat16),
            pltpu.VMEM((t, dv), jnp.float32),
            pltpu.VMEM((nc, 1, dk), jnp.float32),
            pltpu.VMEM((nc, dv, dk), jnp.float32),
            pltpu.VMEM((nc, dv, dk), jnp.bfloat16),
        ],
        compiler_params=_params("parallel", "parallel"),
        name="gla",
    )(proj, proj, proj, proj, glr, wup_pad, b_gate, norm_g)


MASK_VALUE = -1e30


def _moba_kernel(q_ref, k_ref, v_ref, qg_ref, kg_ref, cos_ref, sa_ref, sb_ref, o_ref,
                 qa_ref, ka_ref, vb_ref, bias_ref):
    t, hd = q_ref.shape
    blk = MOBA_BLOCK
    nb = t // blk
    half = ROPE_DIMS // 2
    bf16 = jnp.bfloat16

    def prep(x, g, scale):
        r = lax.rsqrt(jnp.mean(x * x, axis=-1, keepdims=True) + EPS)
        xn = (x * r) * g
        y = (xn * cos_ref[...] + pltpu.roll(xn, half, 1) * sa_ref[...]
             + pltpu.roll(xn, hd - half, 1) * sb_ref[...])
        return y * scale

    qh = prep(q_ref[...], qg_ref[...], hd ** -0.5)
    kh = prep(k_ref[...], kg_ref[...], 1.0)
    vb_ref[...] = v_ref[...].astype(bf16)

    km = jnp.sum(kh.reshape(nb, blk, hd), axis=1) * (1.0 / blk)
    gate_t = _dot_nt(km, qh, _HI)
    kblk = lax.broadcasted_iota(jnp.int32, (nb, t), 0)
    qblk = lax.broadcasted_iota(jnp.int32, (nb, t), 1) // blk
    past = kblk < qblk
    gm = jnp.where(past, gate_t, _NEG_INF)
    rank = jnp.zeros((nb, t), jnp.float32)
    for n2 in range(nb):
        other = gm[n2:n2 + 1, :]
        beats = (other > gm) | ((other == gm) & (kblk > n2))
        rank = rank + beats.astype(jnp.float32)
    allowed = (past & (rank < float(MOBA_TOPK))) | (kblk == qblk)
    bias_ref[...] = jnp.zeros_like(bias_ref)
    bias_ref[0:nb, :] = jnp.where(allowed, 0.0, MASK_VALUE)

    qa_ref[:, 0:hd] = qh.astype(bf16)
    qa_ref[:, hd:2 * hd] = bias_ref[...].T.astype(bf16)
    ka_ref[:, 0:hd] = kh.astype(bf16)
    lane = lax.broadcasted_iota(jnp.int32, (t, hd), 1)
    rblk = lax.broadcasted_iota(jnp.int32, (t, hd), 0) // blk
    ka_ref[:, hd:2 * hd] = (lane == rblk).astype(bf16)

    lrow = lax.broadcasted_iota(jnp.int32, (blk, blk), 0)
    lcol = lax.broadcasted_iota(jnp.int32, (blk, blk), 1)
    causal = lrow >= lcol
    for j in range(nb):
        rows = pl.ds(j * blk, blk)
        nkeys = (j + 1) * blk
        s = _dot_nt(qa_ref[rows, :], ka_ref[0:nkeys, :])
        own = jnp.where(causal, s[:, j * blk:nkeys], MASK_VALUE)
        s = own if j == 0 else jnp.concatenate([s[:, :j * blk], own], axis=1)
        m = jnp.max(s, axis=-1, keepdims=True)
        p = jnp.exp(s - m)
        l = jnp.sum(p, axis=-1, keepdims=True)
        o = _dot(p.astype(bf16), vb_ref[0:nkeys, :])
        o_ref[rows, :] = (o * (1.0 / l)).astype(o_ref.dtype)


def _moba(proj, q_g, k_g, cos_t, sa_t, sb_t, batch, t, col0):
    n = proj.shape[0]
    h, hd = MOBA_HEADS, MOBA_HD
    qb = col0 // hd
    kb = qb + h
    vb = kb + h
    tab = pl.BlockSpec((t, hd), lambda b, i: (0, 0))
    return pl.pallas_call(
        _moba_kernel,
        grid=(batch, h),
        in_specs=[
            pl.BlockSpec((t, hd), lambda b, i: (b, qb + i)),
            pl.BlockSpec((t, hd), lambda b, i: (b, kb + i)),
            pl.BlockSpec((t, hd), lambda b, i: (b, vb + i)),
            pl.BlockSpec((1, hd), lambda b, i: (0, 0)),
            pl.BlockSpec((1, hd), lambda b, i: (0, 0)),
            tab, tab, tab,
        ],
        out_specs=pl.BlockSpec((t, hd), lambda b, i: (b, i)),
        out_shape=jax.ShapeDtypeStruct((n, h * hd), jnp.bfloat16),
        scratch_shapes=[
            pltpu.VMEM((t, 2 * hd), jnp.bfloat16),
            pltpu.VMEM((t, 2 * hd), jnp.bfloat16),
            pltpu.VMEM((t, hd), jnp.bfloat16),
            pltpu.VMEM((LANES, t), jnp.float32),
        ],
        compiler_params=_params("parallel", "parallel"),
        name="moba",
    )(proj, proj, proj, q_g, k_g, cos_t, sa_t, sb_t)


def _out_proj_kernel(x_ref, a_ref, b_ref, wa_ref, wb_ref, o_ref):
    o_ref[...] = x_ref[...] + _dot(a_ref[...], wa_ref[...]) + _dot(b_ref[...], wb_ref[...])


def _out_proj(x2, o_gla, o_moba, w):
    n, d = x2.shape
    ka, kb = o_gla.shape[1], o_moba.shape[1]
    assert ka == kb and w.shape[0] == ka + kb
    tm = _tile(n, 512)
    tn = _tile(d, 1024)
    return pl.pallas_call(
        _out_proj_kernel,
        grid=(n // tm, d // tn),
        in_specs=[
            pl.BlockSpec((tm, tn), lambda i, j: (i, j)),
            pl.BlockSpec((tm, ka), lambda i, j: (i, 0)),
            pl.BlockSpec((tm, kb), lambda i, j: (i, 0)),
            pl.BlockSpec((ka, tn), lambda i, j: (0, j)),
            pl.BlockSpec((kb, tn), lambda i, j: (1, j)),
        ],
        out_specs=pl.BlockSpec((tm, tn), lambda i, j: (i, j)),
        out_shape=jax.ShapeDtypeStruct((n, d), jnp.float32),
        compiler_params=_params("parallel", "parallel"),
        name="out_proj",
    )(x2, o_gla, o_moba, w, w)


def _peer_q_kernel(x_ref, g_ref, w_ref, keys_ref, xn_ref, st_ref, hb_ref):
    @pl.when(pl.program_id(1) == 0)
    def _():
        x = x_ref[...]
        r = lax.rsqrt(jnp.mean(x * x, axis=-1, keepdims=True) + EPS)
        hb = ((x * r) * g_ref[...]).astype(jnp.bfloat16)
        hb_ref[...] = hb
        xn_ref[...] = hb

    q = _dot(hb_ref[...], w_ref[...])
    half = keys_ref.shape[3]
    for hh in range(keys_ref.shape[0]):
        for p in range(2):
            c0 = (2 * hh + p) * half
            st_ref[hh, p] = _dot_nt(keys_ref[hh, p], q[:, c0:c0 + half], _HI)


PEER_Q_HEADS_PER_STEP = 2


def _peer_q(x1, g, w_pq, keys):
    n, d = x1.shape
    h, dq, nk = PEER_HEADS, PEER_DQ, PEER_NKEYS
    hp = math.gcd(h, PEER_Q_HEADS_PER_STEP)
    tm = _tile(n, 512)
    return pl.pallas_call(
        _peer_q_kernel,
        grid=(n // tm, h // hp),
        in_specs=[
            pl.BlockSpec((tm, d), lambda i, j: (i, 0)),
            pl.BlockSpec((1, d), lambda i, j: (0, 0)),
            pl.BlockSpec((d, hp * dq), lambda i, j: (0, j)),
            pl.BlockSpec((hp, 2, nk, dq // 2), lambda i, j: (j, 0, 0, 0)),
        ],
        out_specs=[
            pl.BlockSpec((tm, d), lambda i, j: (i, 0)),
            pl.BlockSpec((hp, 2, nk, tm), lambda i, j: (j, 0, 0, i)),
        ],
        out_shape=[
            jax.ShapeDtypeStruct((n, d), jnp.bfloat16),
            jax.ShapeDtypeStruct((h, 2, nk, n), jnp.float32),
        ],
        scratch_shapes=[pltpu.VMEM((tm, d), jnp.bfloat16)],
        compiler_params=_params("parallel", "arbitrary"),
        name="peer_q",
    )(x1, g, w_pq, keys)


def _extract_topk(cur, k, out_ref=None):
    rows = cur.shape[0]
    iota = lax.broadcasted_iota(jnp.int32, cur.shape, 0)
    mk = None
    for r in range(k):
        mk = jnp.max(cur, axis=0, keepdims=True)
        if out_ref is not None:
            out_ref[r:r + 1, :] = mk
        first = jnp.min(jnp.where(cur == mk, iota, rows), axis=0, keepdims=True)
        cur = jnp.where(iota == first, _NEG_INF, cur)
    return cur, mk


def _peer_topk_kernel(st_ref, s1m_ref, e1_ref, s2m_ref, e2n_ref, tau_ref, v1_ref, v2_ref):
    k = PEER_TOPK
    s1 = st_ref[0, 0]
    s2 = st_ref[0, 1]
    rem1, _ = _extract_topk(s1, k, v1_ref)
    rem2, _ = _extract_topk(s2, k, v2_ref)
    s1m_ref[0] = jnp.where(rem1 != s1, s1, _NEG_INF)
    s2m = jnp.where(rem2 != s2, s2, _NEG_INF)
    s2m_ref[0] = s2m
    v1 = v1_ref[...]
    v2 = v2_ref[...]
    m1 = v1[0:1, :]
    m2 = v2[0:1, :]
    e1_ref[0] = jnp.exp(s1 - m1)
    e2 = jnp.exp(s2 - m2)
    ev1 = jnp.exp(v1 - m1)
    ev2 = jnp.exp(v2 - m2)

    r8 = lax.broadcasted_iota(jnp.int32, (SUBLANES, v1.shape[1]), 0)
    r16 = lax.broadcasted_iota(jnp.int32, (2 * SUBLANES, v1.shape[1]), 0)

    def pieces(a, b, comb):
        lo_b, lo_a = b[0:SUBLANES, :], a[0:SUBLANES, :]
        return [
            (comb(a[0:1, :], b), None),
            (comb(a[1:2, :], lo_b), None),
            (comb(a[2:3, :], lo_b), r8 < 5),
            (comb(a[3:4, :], lo_b), r8 < 4),
            (comb(a, b[0:1, :]), r16 >= 4),
            (comb(lo_a, b[1:2, :]), r8 >= 4),
            (comb(lo_a, b[2:3, :]), r8 == 4),
        ]

    cand = jnp.concatenate(
        [c if msk is None else jnp.where(msk, c, _NEG_INF)
         for c, msk in pieces(v1, v2, lambda x, y: x + y)], axis=0)
    prod = jnp.concatenate([c for c, _ in pieces(ev1, ev2, lambda x, y: x * y)], axis=0)
    _, tau = _extract_topk(cand, k)
    zsum = jnp.sum(jnp.where(cand >= tau, prod, 0.0), axis=0, keepdims=True)
    e2n_ref[0] = e2 * (1.0 / zsum)
    tau_ref[0] = tau


def _peer_topk(st):
    h, _, nk, n = st.shape
    tl = _tile(n, 256)
    big = pl.BlockSpec((1, nk, tl), lambda i, j: (j, 0, i))
    shp = jax.ShapeDtypeStruct((h, nk, n), jnp.float32)
    return pl.pallas_call(
        _peer_topk_kernel,
        grid=(n // tl, h),
        in_specs=[pl.BlockSpec((1, 2, nk, tl), lambda i, j: (j, 0, 0, i))],
        out_specs=[big, big, big, big, pl.BlockSpec((1, 1, tl), lambda i, j: (j, 0, i))],
        out_shape=[shp, shp, shp, shp, jax.ShapeDtypeStruct((h, 1, n), jnp.float32)],
        scratch_shapes=[pltpu.VMEM((PEER_TOPK, tl), jnp.float32),
                        pltpu.VMEM((PEER_TOPK, tl), jnp.float32)],
        compiler_params=_params("parallel", "parallel"),
        name="peer_topk",
    )(st)


def _peer_ffn_kernel(x1_ref, xn_ref, u_ref, v_ref, s1m_ref, e1_ref, s2m_ref, e2n_ref, tau_ref, o_ref,
                     *, rows_per_step):
    e = pl.program_id(1)
    nh = s2m_ref.shape[0]
    nk = s2m_ref.shape[1]
    rblk = s1m_ref.shape[1]

    @pl.when(e == 0)
    def _():
        o_ref[...] = x1_ref[...]

    tm, d = xn_ref.shape
    base = (e % (rblk // rows_per_step)) * rows_per_step
    slabs = []
    zeros = []
    half_word = jnp.uint32(16)
    for r in range(rows_per_step):
        i8 = base + r
        acc = jnp.zeros((nk, tm), jnp.float32)
        for h in range(nh):
            ssum = s1m_ref[h, pl.ds(i8, 1), :] + s2m_ref[h]
            w = e1_ref[h, pl.ds(i8, 1), :] * e2n_ref[h]
            acc = acc + jnp.where(ssum >= tau_ref[h], w, 0.0)
        slabs.append(acc.T)
        bits = lax.bitcast_convert_type(acc, jnp.uint32)
        zf = lax.bitcast_convert_type(
            lax.shift_right_logical(lax.shift_right_logical(bits, half_word), half_word), jnp.float32)
        for c in range(tm // LANES):
            for g in range(nk // SUBLANES):
                blk = zf[g * SUBLANES:(g + 1) * SUBLANES, c * LANES:(c + 1) * LANES]
                zeros.append(jnp.concatenate([blk, blk], axis=0).astype(jnp.bfloat16))
    wsel = jnp.concatenate(slabs, axis=1) if len(slabs) > 1 else slabs[0]

    rb = 2 * SUBLANES
    kw = 2 * LANES
    n_pairs = (tm // rb) * (d // kw)
    cols = []
    for kb in range(d // kw):
        rows = []
        for mb in range(tm // rb):
            zb = zeros[((kb * (tm // rb) + mb) * len(zeros)) // n_pairs]
            rows.append(xn_ref[mb * rb:(mb + 1) * rb, kb * kw:(kb + 1) * kw] + jnp.concatenate([zb, zb], axis=1))
        cols.append(jnp.concatenate(rows, axis=0))
    hmat = _dot_nt(jnp.concatenate(cols, axis=1), u_ref[...])
    act = 0.5 * hmat * (1.0 + lax.erf(hmat * (1.0 / math.sqrt(2.0))))
    z = jnp.where(wsel != 0.0, wsel * act, 0.0).astype(jnp.bfloat16)
    o_ref[...] += _dot(z, v_ref[...])


def _peer_ffn(x1, xn, u_b, v_b, s1m, e1, s2m, e2n, tau):
    n, d = x1.shape
    ne = u_b.shape[0]
    nh, nk, _ = s1m.shape
    tm = _tile(n, 512)
    te = _tile(ne, 512)
    rows_per_step = te // nk
    rblk = max(SUBLANES, rows_per_step)
    steps_per_rblk = rblk // rows_per_step
    once = dict(pipeline_mode=pl.Buffered(1))
    return pl.pallas_call(
        functools.partial(_peer_ffn_kernel, rows_per_step=rows_per_step),
        grid=(n // tm, ne // te),
        in_specs=[
            pl.BlockSpec((tm, d), lambda i, j: (i, 0), **once),
            pl.BlockSpec((tm, d), lambda i, j: (i, 0), **once),
            pl.BlockSpec((te, d), lambda i, j: (j, 0)),
            pl.BlockSpec((te, d), lambda i, j: (j, 0)),
            pl.BlockSpec((nh, rblk, tm), lambda i, j: (0, j // steps_per_rblk, i)),
            pl.BlockSpec((nh, rblk, tm), lambda i, j: (0, j // steps_per_rblk, i)),
            pl.BlockSpec((nh, nk, tm), lambda i, j: (0, 0, i), **once),
            pl.BlockSpec((nh, nk, tm), lambda i, j: (0, 0, i), **once),
            pl.BlockSpec((nh, 1, tm), lambda i, j: (0, 0, i)),
        ],
        out_specs=pl.BlockSpec((tm, d), lambda i, j: (i, 0)),
        out_shape=jax.ShapeDtypeStruct((n, d), jnp.float32),
        compiler_params=_params("parallel", "arbitrary"),
        name="peer_ffn",
    )(x1, xn, u_b, v_b, s1m, e1, s2m, e2n, tau)


def _rope_tables(t, hd):
    half = ROPE_DIMS // 2
    inv = ROPE_THETA ** (-jnp.arange(half, dtype=jnp.float32) / half)
    ang = jnp.arange(t).astype(jnp.float32)[:, None] * inv[None, :]
    cos, sin = jnp.cos(ang), jnp.sin(ang)
    zeros = jnp.zeros((t, hd - 2 * half), jnp.float32)
    z16 = jnp.zeros((t, half), jnp.float32)
    cos_t = jnp.concatenate([cos, cos, jnp.ones_like(zeros)], axis=1)
    sa_t = jnp.concatenate([z16, sin, zeros], axis=1)
    sb_t = jnp.concatenate([-sin, z16, zeros], axis=1)
    return cos_t, sa_t, sb_t


def kernel(x, norm_mix_g, w_in, w_gate_up, b_gate, gla_norm_g, q_norm_g, k_norm_g, w_out, norm_ffn_g,
           peer_wq, peer_keys, peer_u, peer_v):
    batch, t, d = x.shape
    n = batch * t
    bf16 = jnp.bfloat16
    gla_qk_w = GLA_HEADS * GLA_DK
    gla_v_w = GLA_HEADS * GLA_DV
    lr0 = 2 * gla_qk_w + 2 * gla_v_w
    cos_t, sa_t, sb_t = _rope_tables(t, MOBA_HD)

    xcur = x.reshape(n, d)
    for l in range(w_in.shape[0]):
        wl = w_in[l]
        w_main = jnp.concatenate([wl[:, :lr0], wl[:, lr0 + GLA_GATE_RANK:]], axis=1).astype(bf16)
        w_lr = jnp.pad(wl[:, lr0:lr0 + GLA_GATE_RANK], ((0, 0), (0, LANES - GLA_GATE_RANK))).astype(bf16)
        wup_pad = jnp.pad(w_gate_up[l], ((0, LANES - GLA_GATE_RANK), (0, 0)))

        proj, glr = _in_proj(xcur, norm_mix_g[l][None, :], w_main, w_lr)
        o_gla = _gla(proj, glr, wup_pad, b_gate[l][None, :], gla_norm_g[l][None, :], batch, t)
        o_moba = _moba(proj, q_norm_g[l][None, :], k_norm_g[l][None, :], cos_t, sa_t, sb_t, batch, t, lr0)
        x1 = _out_proj(xcur, o_gla, o_moba, w_out[l].astype(bf16))

        xn, st = _peer_q(x1, norm_ffn_g[l][None, :], peer_wq[l].astype(bf16), peer_keys[l])
        s1m, e1, s2m, e2n, tau = _peer_topk(st)
        xcur = _peer_ffn(x1, xn, peer_u[l].astype(bf16), peer_v[l].astype(bf16), s1m, e1, s2m, e2n, tau)
    return xcur.reshape(batch, t, d)
```

```python
import functools
import math

import jax
import jax.numpy as jnp
from jax import lax
from jax.experimental import pallas as pl
from jax.experimental.pallas import tpu as pltpu

GLA_HEADS = 8
GLA_DK = 128
GLA_DV = 256
GLA_GATE_RANK = 16
GLA_GATE_NORMALIZER = 16.0
GLA_CHUNK = 64
MOBA_HEADS = 16
MOBA_HD = 128
MOBA_BLOCK = 256
MOBA_TOPK = 3
ROPE_THETA = 500000.0
ROPE_DIMS = MOBA_HD // 4
PEER_HEADS = 8
PEER_NKEYS = 128
PEER_DQ = 256
PEER_TOPK = 16
EPS = 1e-6

LANES = 128
SUBLANES = 8
VMEM_LIMIT_BYTES = 60 * 1024 * 1024

_HI = lax.Precision.HIGHEST
_NEG_INF = float("-inf")


def _dot(a, b, precision=None):
    return lax.dot_general(a, b, (((1,), (0,)), ((), ())), precision=precision,
                           preferred_element_type=jnp.float32)


def _dot_nt(a, b, precision=None):
    return lax.dot_general(a, b, (((1,), (1,)), ((), ())), precision=precision,
                           preferred_element_type=jnp.float32)


def _dot_tn(a, b, precision=None):
    return lax.dot_general(a, b, (((0,), (0,)), ((), ())), precision=precision,
                           preferred_element_type=jnp.float32)


def _tile(n, pref):
    t = min(n, pref)
    while n % t:
        t //= 2
    return t


def _params(*sem, flags=None):
    return pltpu.CompilerParams(dimension_semantics=sem, vmem_limit_bytes=VMEM_LIMIT_BYTES, flags=flags)


def _in_proj_kernel(x_ref, g_ref, wt_ref, wlrt_ref, o_ref, olr_ref, hb_ref):
    @pl.when(pl.program_id(1) == 0)
    def _():
        x = x_ref[...]
        r = lax.rsqrt(jnp.mean(x * x, axis=-1, keepdims=True) + EPS)
        hb = ((x * r) * g_ref[...]).astype(jnp.bfloat16)
        hb_ref[...] = hb
        olr_ref[...] = _dot_nt(hb, wlrt_ref[...])

    o_ref[...] = _dot_nt(hb_ref[...], wt_ref[...])


def _in_proj(x2, g, wt, wlr_t, split, skip):
    n, d = x2.shape
    cols = wt.shape[0] - skip
    tm = _tile(n, 512)
    tn = _tile(math.gcd(split, cols - split), 1024)
    return pl.pallas_call(
        _in_proj_kernel,
        grid=(n // tm, cols // tn),
        in_specs=[
            pl.BlockSpec((tm, d), lambda i, j: (i, 0)),
            pl.BlockSpec((1, d), lambda i, j: (0, 0)),
            pl.BlockSpec((pl.Element(tn), pl.Element(d)),
                         lambda i, j: (pl.multiple_of(j * tn + jnp.where(j * tn >= split, skip, 0), skip), 0)),
            pl.BlockSpec((LANES, d), lambda i, j: (0, 0)),
        ],
        out_specs=[
            pl.BlockSpec((tm, tn), lambda i, j: (i, j)),
            pl.BlockSpec((tm, LANES), lambda i, j: (i, 0)),
        ],
        out_shape=[
            jax.ShapeDtypeStruct((n, cols), jnp.float32),
            jax.ShapeDtypeStruct((n, LANES), jnp.float32),
        ],
        scratch_shapes=[pltpu.VMEM((tm, d), jnp.bfloat16)],
        compiler_params=_params("parallel", "arbitrary"),
        name="in_proj",
    )(x2, g, wt, wlr_t)


GLA_GROUP = 256


def _gla_kernel(q_ref, k_ref, v_ref, go_ref, glr_ref, wup_ref, bg_ref, ng_ref, o_ref,
                qd_ref, ki_ref, ks_ref, vb_ref, oacc_ref, dec_ref, upd_ref, sprev_ref):
    t, dk = q_ref.shape
    dv = v_ref.shape[1]
    c = GLA_CHUNK
    n = t // c
    grp = min(GLA_GROUP, t)
    scale = dk ** -0.5
    bf16 = jnp.bfloat16

    z = _dot(glr_ref[...], wup_ref[...], _HI) + bg_ref[...]
    log_a = (jnp.minimum(z, 0.0) - jnp.log1p(jnp.exp(-jnp.abs(z)))) / GLA_GATE_NORMALIZER
    row = lax.broadcasted_iota(jnp.int32, (grp, grp), 0)
    col = lax.broadcasted_iota(jnp.int32, (grp, grp), 1)
    same_chunk_causal = (row >= col) & ((row // c) == (col // c))
    tril = same_chunk_causal.astype(jnp.float32)
    bc = jnp.concatenate([_dot(tril, log_a[g * grp:(g + 1) * grp, :], _HI) for g in range(t // grp)], axis=0)
    bc3 = bc.reshape(n, c, dk)
    bl3 = bc3[:, c - 1:c, :]
    k3 = k_ref[...].reshape(n, c, dk)
    qd_ref[...] = (q_ref[...] * scale * jnp.exp(bc)).astype(bf16)
    ki_ref[...] = (k_ref[...] * jnp.exp(-bc)).astype(bf16)
    ks_ref[...] = (k3 * jnp.exp(bl3 - bc3)).reshape(t, dk).astype(bf16)
    dec_ref[...] = jnp.exp(bl3)
    vb_ref[...] = v_ref[...].astype(bf16)

    for g in range(t // grp):
        rows = pl.ds(g * grp, grp)
        att = jnp.where(same_chunk_causal, _dot_nt(qd_ref[rows, :], ki_ref[rows, :]), 0.0)
        oacc_ref[rows, :] = _dot(att.astype(bf16), vb_ref[rows, :])

    def upd_body(ci, carry):
        rows = pl.ds(pl.multiple_of(ci * c, c), c)
        upd_ref[ci] = _dot_tn(vb_ref[rows, :], ks_ref[rows, :])
        return carry

    lax.fori_loop(0, n, upd_body, 0, unroll=4)

    def scan_body(ci, st):
        sprev_ref[ci] = st.astype(bf16)
        return dec_ref[ci] * st + upd_ref[ci]

    lax.fori_loop(0, n, scan_body, jnp.zeros((dv, dk), jnp.float32), unroll=2)

    def inter_body(ci, carry):
        rows = pl.ds(pl.multiple_of(ci * c, c), c)
        oacc_ref[rows, :] += _dot_nt(qd_ref[rows, :], sprev_ref[ci])
        return carry

    lax.fori_loop(0, n, inter_body, 0, unroll=4)

    o = oacc_ref[...]
    r = lax.rsqrt(jnp.mean(o * o, axis=-1, keepdims=True) + EPS)
    go = go_ref[...]
    o_ref[...] = (((o * r) * ng_ref[...]) * (go * jax.nn.sigmoid(go))).astype(o_ref.dtype)


def _gla(proj, glr, wup_pad, b_gate, norm_g, batch, t):
    n = proj.shape[0]
    h, dk, dv = GLA_HEADS, GLA_DK, GLA_DV
    nc = t // GLA_CHUNK
    qk_w = h * dk
    kb = qk_w // dk
    vb = (2 * qk_w) // dv
    gb = (2 * qk_w + h * dv) // dv
    return pl.pallas_call(
        _gla_kernel,
        grid=(batch, h),
        in_specs=[
            pl.BlockSpec((t, dk), lambda b, i: (b, i)),
            pl.BlockSpec((t, dk), lambda b, i: (b, kb + i)),
            pl.BlockSpec((t, dv), lambda b, i: (b, vb + i)),
            pl.BlockSpec((t, dv), lambda b, i: (b, gb + i)),
            pl.BlockSpec((t, LANES), lambda b, i: (b, 0)),
            pl.BlockSpec((LANES, dk), lambda b, i: (0, i)),
            pl.BlockSpec((1, dk), lambda b, i: (0, i)),
            pl.BlockSpec((1, dv), lambda b, i: (0, 0)),
        ],
        out_specs=pl.BlockSpec((t, dv), lambda b, i: (b, i)),
        out_shape=jax.ShapeDtypeStruct((n, h * dv), jnp.bfloat16),
        scratch_shapes=[
            pltpu.VMEM((t, dk), jnp.bfloat16),
            pltpu.VMEM((t, dk), jnp.bfloat16),
            pltpu.VMEM((t, dk), jnp.bfloat16),
            pltpu.VMEM((t, dv), jnp.bfloat16),
            pltpu.VMEM((t, dv), jnp.float32),
            pltpu.VMEM((nc, 1, dk), jnp.float32),
            pltpu.VMEM((nc, dv, dk), jnp.float32),
            pltpu.VMEM((nc, dv, dk), jnp.bfloat16),
        ],
        compiler_params=_params("parallel", "parallel"),
        name="gla",
    )(proj, proj, proj, proj, glr, wup_pad, b_gate, norm_g)


MASK_VALUE = -1e30


def _moba_kernel(q_ref, k_ref, v_ref, qg_ref, kg_ref, cos_ref, sa_ref, sb_ref, o_ref,
                 qa_ref, ka_ref, vb_ref, bias_ref):
    t, hd = q_ref.shape
    blk = MOBA_BLOCK
    nb = t // blk
    half = ROPE_DIMS // 2
    bf16 = jnp.bfloat16

    def prep(x, g, scale):
        r = lax.rsqrt(jnp.mean(x * x, axis=-1, keepdims=True) + EPS)
        xn = (x * r) * g
        y = (xn * cos_ref[...] + pltpu.roll(xn, half, 1) * sa_ref[...]
             + pltpu.roll(xn, hd - half, 1) * sb_ref[...])
        return y * scale

    qh = prep(q_ref[...], qg_ref[...], hd ** -0.5)
    kh = prep(k_ref[...], kg_ref[...], 1.0)
    vb_ref[...] = v_ref[...].astype(bf16)

    km = jnp.sum(kh.reshape(nb, blk, hd), axis=1) * (1.0 / blk)
    gate_t = _dot_nt(km, qh, _HI)
    kblk = lax.broadcasted_iota(jnp.int32, (nb, t), 0)
    qblk = lax.broadcasted_iota(jnp.int32, (nb, t), 1) // blk
    past = kblk < qblk
    gm = jnp.where(past, gate_t, _NEG_INF)
    rank = jnp.zeros((nb, t), jnp.float32)
    for n2 in range(nb):
        other = gm[n2:n2 + 1, :]
        beats = (other > gm) | ((other == gm) & (kblk > n2))
        rank = rank + beats.astype(jnp.float32)
    allowed = (past & (rank < float(MOBA_TOPK))) | (kblk == qblk)
    bias_ref[...] = jnp.zeros_like(bias_ref)
    bias_ref[0:nb, :] = jnp.where(allowed, 0.0, MASK_VALUE)

    qa_ref[:, 0:hd] = qh.astype(bf16)
    qa_ref[:, hd:2 * hd] = bias_ref[...].T.astype(bf16)
    ka_ref[:, 0:hd] = kh.astype(bf16)
    lane = lax.broadcasted_iota(jnp.int32, (t, hd), 1)
    rblk = lax.broadcasted_iota(jnp.int32, (t, hd), 0) // blk
    ka_ref[:, hd:2 * hd] = (lane == rblk).astype(bf16)

    lrow = lax.broadcasted_iota(jnp.int32, (blk, blk), 0)
    lcol = lax.broadcasted_iota(jnp.int32, (blk, blk), 1)
    causal = lrow >= lcol
    for j in range(nb):
        rows = pl.ds(j * blk, blk)
        nkeys = (j + 1) * blk
        s = _dot_nt(qa_ref[rows, :], ka_ref[0:nkeys, :])
        own = jnp.where(causal, s[:, j * blk:nkeys], MASK_VALUE)
        s = own if j == 0 else jnp.concatenate([s[:, :j * blk], own], axis=1)
        m = jnp.max(s, axis=-1, keepdims=True)
        p = jnp.exp(s - m)
        l = jnp.sum(p, axis=-1, keepdims=True)
        o = _dot(p.astype(bf16), vb_ref[0:nkeys, :])
        o_ref[rows, :] = (o * (1.0 / l)).astype(o_ref.dtype)


def _moba(proj, q_g, k_g, cos_t, sa_t, sb_t, batch, t, col0):
    n = proj.shape[0]
    h, hd = MOBA_HEADS, MOBA_HD
    qb = col0 // hd
    kb = qb + h
    vb = kb + h
    tab = pl.BlockSpec((t, hd), lambda b, i: (0, 0))
    return pl.pallas_call(
        _moba_kernel,
        grid=(batch, h),
        in_specs=[
            pl.BlockSpec((t, hd), lambda b, i: (b, qb + i)),
            pl.BlockSpec((t, hd), lambda b, i: (b, kb + i)),
            pl.BlockSpec((t, hd), lambda b, i: (b, vb + i)),
            pl.BlockSpec((1, hd), lambda b, i: (0, 0)),
            pl.BlockSpec((1, hd), lambda b, i: (0, 0)),
            tab, tab, tab,
        ],
        out_specs=pl.BlockSpec((t, hd), lambda b, i: (b, i)),
        out_shape=jax.ShapeDtypeStruct((n, h * hd), jnp.bfloat16),
        scratch_shapes=[
            pltpu.VMEM((t, 2 * hd), jnp.bfloat16),
            pltpu.VMEM((t, 2 * hd), jnp.bfloat16),
            pltpu.VMEM((t, hd), jnp.bfloat16),
            pltpu.VMEM((LANES, t), jnp.float32),
        ],
        compiler_params=_params("parallel", "parallel"),
        name="moba",
    )(proj, proj, proj, q_g, k_g, cos_t, sa_t, sb_t)


def _out_proj_kernel(x_ref, a_ref, b_ref, wa_ref, wb_ref, o_ref):
    o_ref[...] = x_ref[...] + _dot(a_ref[...], wa_ref[...]) + _dot(b_ref[...], wb_ref[...])


def _out_proj(x2, o_gla, o_moba, w):
    n, d = x2.shape
    ka, kb = o_gla.shape[1], o_moba.shape[1]
    assert ka == kb and w.shape[0] == ka + kb
    tm = _tile(n, 512)
    tn = _tile(d, 1024)
    return pl.pallas_call(
        _out_proj_kernel,
        grid=(n // tm, d // tn),
        in_specs=[
            pl.BlockSpec((tm, tn), lambda i, j: (i, j)),
            pl.BlockSpec((tm, ka), lambda i, j: (i, 0)),
            pl.BlockSpec((tm, kb), lambda i, j: (i, 0)),
            pl.BlockSpec((ka, tn), lambda i, j: (0, j)),
            pl.BlockSpec((kb, tn), lambda i, j: (1, j)),
        ],
        out_specs=pl.BlockSpec((tm, tn), lambda i, j: (i, j)),
        out_shape=jax.ShapeDtypeStruct((n, d), jnp.float32),
        compiler_params=_params("parallel", "parallel"),
        name="out_proj",
    )(x2, o_gla, o_moba, w, w)


def _peer_q_kernel(x_ref, g_ref, w_ref, keys_ref, xn_ref, st_ref, hb_ref):
    @pl.when(pl.program_id(1) == 0)
    def _():
        x = x_ref[...]
        r = lax.rsqrt(jnp.mean(x * x, axis=-1, keepdims=True) + EPS)
        hb = ((x * r) * g_ref[...]).astype(jnp.bfloat16)
        hb_ref[...] = hb
        xn_ref[...] = hb

    q = _dot(hb_ref[...], w_ref[...])
    half = keys_ref.shape[3]
    for hh in range(keys_ref.shape[0]):
        for p in range(2):
            c0 = (2 * hh + p) * half
            st_ref[hh, p] = _dot_nt(keys_ref[hh, p], q[:, c0:c0 + half], _HI)


PEER_Q_HEADS_PER_STEP = 2


def _peer_q(x1, g, w_pq, keys):
    n, d = x1.shape
    h, dq, nk = PEER_HEADS, PEER_DQ, PEER_NKEYS
    hp = math.gcd(h, PEER_Q_HEADS_PER_STEP)
    tm = _tile(n, 512)
    return pl.pallas_call(
        _peer_q_kernel,
        grid=(n // tm, h // hp),
        in_specs=[
            pl.BlockSpec((tm, d), lambda i, j: (i, 0)),
            pl.BlockSpec((1, d), lambda i, j: (0, 0)),
            pl.BlockSpec((d, hp * dq), lambda i, j: (0, j)),
            pl.BlockSpec((hp, 2, nk, dq // 2), lambda i, j: (j, 0, 0, 0)),
        ],
        out_specs=[
            pl.BlockSpec((tm, d), lambda i, j: (i, 0)),
            pl.BlockSpec((hp, 2, nk, tm), lambda i, j: (j, 0, 0, i)),
        ],
        out_shape=[
            jax.ShapeDtypeStruct((n, d), jnp.bfloat16),
            jax.ShapeDtypeStruct((h, 2, nk, n), jnp.float32),
        ],
        scratch_shapes=[pltpu.VMEM((tm, d), jnp.bfloat16)],
        compiler_params=_params("parallel", "arbitrary"),
        name="peer_q",
    )(x1, g, w_pq, keys)


def _extract_topk(cur, k, out_ref=None):
    rows = cur.shape[0]
    iota = lax.broadcasted_iota(jnp.int32, cur.shape, 0)
    mk = None
    for r in range(k):
        mk = jnp.max(cur, axis=0, keepdims=True)
        if out_ref is not None:
            out_ref[r:r + 1, :] = mk
        first = jnp.min(jnp.where(cur == mk, iota, rows), axis=0, keepdims=True)
        cur = jnp.where(iota == first, _NEG_INF, cur)
    return cur, mk


def _peer_topk_kernel(st_ref, s1m_ref, e1_ref, s2m_ref, e2n_ref, tau_ref, v1_ref, v2_ref):
    k = PEER_TOPK
    s1 = st_ref[0, 0]
    s2 = st_ref[0, 1]
    rem1, _ = _extract_topk(s1, k, v1_ref)
    rem2, _ = _extract_topk(s2, k, v2_ref)
    s1m_ref[0] = jnp.where(rem1 != s1, s1, _NEG_INF)
    s2m = jnp.where(rem2 != s2, s2, _NEG_INF)
    s2m_ref[0] = s2m
    v1 = v1_ref[...]
    v2 = v2_ref[...]
    m1 = v1[0:1, :]
    m2 = v2[0:1, :]
    e1_ref[0] = jnp.exp(s1 - m1)
    e2 = jnp.exp(s2 - m2)
    ev1 = jnp.exp(v1 - m1)
    ev2 = jnp.exp(v2 - m2)

    r8 = lax.broadcasted_iota(jnp.int32, (SUBLANES, v1.shape[1]), 0)
    r16 = lax.broadcasted_iota(jnp.int32, (2 * SUBLANES, v1.shape[1]), 0)

    def pieces(a, b, comb):
        lo_b, lo_a = b[0:SUBLANES, :], a[0:SUBLANES, :]
        return [
            (comb(a[0:1, :], b), None),
            (comb(a[1:2, :], lo_b), None),
            (comb(a[2:3, :], lo_b), r8 < 5),
            (comb(a[3:4, :], lo_b), r8 < 4),
            (comb(a, b[0:1, :]), r16 >= 4),
            (comb(lo_a, b[1:2, :]), r8 >= 4),
            (comb(lo_a, b[2:3, :]), r8 == 4),
        ]

    cand = jnp.concatenate(
        [c if msk is None else jnp.where(msk, c, _NEG_INF)
         for c, msk in pieces(v1, v2, lambda x, y: x + y)], axis=0)
    prod = jnp.concatenate([c for c, _ in pieces(ev1, ev2, lambda x, y: x * y)], axis=0)
    _, tau = _extract_topk(cand, k)
    zsum = jnp.sum(jnp.where(cand >= tau, prod, 0.0), axis=0, keepdims=True)
    e2n_ref[0] = e2 * (1.0 / zsum)
    tau_ref[0] = tau


def _peer_topk(st):
    h, _, nk, n = st.shape
    tl = _tile(n, 256)
    big = pl.BlockSpec((1, nk, tl), lambda i, j: (j, 0, i))
    shp = jax.ShapeDtypeStruct((h, nk, n), jnp.float32)
    return pl.pallas_call(
        _peer_topk_kernel,
        grid=(n // tl, h),
        in_specs=[pl.BlockSpec((1, 2, nk, tl), lambda i, j: (j, 0, 0, i))],
        out_specs=[big, big, big, big, pl.BlockSpec((1, 1, tl), lambda i, j: (j, 0, i))],
        out_shape=[shp, shp, shp, shp, jax.ShapeDtypeStruct((h, 1, n), jnp.float32)],
        scratch_shapes=[pltpu.VMEM((PEER_TOPK, tl), jnp.float32),
                        pltpu.VMEM((PEER_TOPK, tl), jnp.float32)],
        compiler_params=_params("parallel", "parallel"),
        name="peer_topk",
    )(st)


def _peer_ffn_kernel(x1_ref, xn_ref, u_ref, v_ref, s1m_ref, e1_ref, s2m_ref, e2n_ref, tau_ref, o_ref,
                     *, rows_per_step):
    e = pl.program_id(1)
    nh = s2m_ref.shape[0]
    nk = s2m_ref.shape[1]
    rblk = s1m_ref.shape[1]

    @pl.when(e == 0)
    def _():
        o_ref[...] = x1_ref[...]

    tm, d = xn_ref.shape
    base = (e % (rblk // rows_per_step)) * rows_per_step
    slabs = []
    zeros = []
    half_word = jnp.uint32(16)
    for r in range(rows_per_step):
        i8 = base + r
        acc = jnp.zeros((nk, tm), jnp.float32)
        for h in range(nh):
            ssum = s1m_ref[h, pl.ds(i8, 1), :] + s2m_ref[h]
            w = e1_ref[h, pl.ds(i8, 1), :] * e2n_ref[h]
            acc = acc + jnp.where(ssum >= tau_ref[h], w, 0.0)
        slabs.append(acc.T)
        bits = lax.bitcast_convert_type(acc, jnp.uint32)
        zf = lax.bitcast_convert_type(
            lax.shift_right_logical(lax.shift_right_logical(bits, half_word), half_word), jnp.float32)
        for c in range(tm // LANES):
            for g in range(nk // SUBLANES):
                blk = zf[g * SUBLANES:(g + 1) * SUBLANES, c * LANES:(c + 1) * LANES]
                zeros.append(jnp.concatenate([blk, blk], axis=0).astype(jnp.bfloat16))
    wsel = jnp.concatenate(slabs, axis=1) if len(slabs) > 1 else slabs[0]

    rb = 2 * SUBLANES
    kw = 2 * LANES
    n_pairs = (tm // rb) * (d // kw)
    cols = []
    for kb in range(d // kw):
        rows = []
        for mb in range(tm // rb):
            zb = zeros[((kb * (tm // rb) + mb) * len(zeros)) // n_pairs]
            rows.append(xn_ref[mb * rb:(mb + 1) * rb, kb * kw:(kb + 1) * kw] + jnp.concatenate([zb, zb], axis=1))
        cols.append(jnp.concatenate(rows, axis=0))
    hmat = _dot_nt(jnp.concatenate(cols, axis=1), u_ref[...])
    act = 0.5 * hmat * (1.0 + lax.erf(hmat * (1.0 / math.sqrt(2.0))))
    z = jnp.where(wsel != 0.0, wsel * act, 0.0).astype(jnp.bfloat16)
    o_ref[...] += _dot(z, v_ref[...])


def _peer_ffn(x1, xn, u_b, v_b, s1m, e1, s2m, e2n, tau):
    n, d = x1.shape
    ne = u_b.shape[0]
    nh, nk, _ = s1m.shape
    tm = _tile(n, 512)
    te = _tile(ne, 512)
    rows_per_step = te // nk
    rblk = max(SUBLANES, rows_per_step)
    steps_per_rblk = rblk // rows_per_step
    once = dict(pipeline_mode=pl.Buffered(1))
    return pl.pallas_call(
        functools.partial(_peer_ffn_kernel, rows_per_step=rows_per_step),
        grid=(n // tm, ne // te),
        in_specs=[
            pl.BlockSpec((tm, d), lambda i, j: (i, 0), **once),
            pl.BlockSpec((tm, d), lambda i, j: (i, 0)),
            pl.BlockSpec((te, d), lambda i, j: (j, 0)),
            pl.BlockSpec((te, d), lambda i, j: (j, 0)),
            pl.BlockSpec((nh, rblk, tm), lambda i, j: (0, j // steps_per_rblk, i)),
            pl.BlockSpec((nh, rblk, tm), lambda i, j: (0, j // steps_per_rblk, i)),
            pl.BlockSpec((nh, nk, tm), lambda i, j: (0, 0, i)),
            pl.BlockSpec((nh, nk, tm), lambda i, j: (0, 0, i)),
            pl.BlockSpec((nh, 1, tm), lambda i, j: (0, 0, i)),
        ],
        out_specs=pl.BlockSpec((tm, d), lambda i, j: (i, 0)),
        out_shape=jax.ShapeDtypeStruct((n, d), jnp.float32),
        compiler_params=_params("parallel", "arbitrary"),
        name="peer_ffn",
    )(x1, xn, u_b, v_b, s1m, e1, s2m, e2n, tau)


def _rope_tables(t, hd):
    half = ROPE_DIMS // 2
    inv = ROPE_THETA ** (-jnp.arange(half, dtype=jnp.float32) / half)
    ang = jnp.arange(t).astype(jnp.float32)[:, None] * inv[None, :]
    cos, sin = jnp.cos(ang), jnp.sin(ang)
    zeros = jnp.zeros((t, hd - 2 * half), jnp.float32)
    z16 = jnp.zeros((t, half), jnp.float32)
    cos_t = jnp.concatenate([cos, cos, jnp.ones_like(zeros)], axis=1)
    sa_t = jnp.concatenate([z16, sin, zeros], axis=1)
    sb_t = jnp.concatenate([-sin, z16, zeros], axis=1)
    return cos_t, sa_t, sb_t


def kernel(x, norm_mix_g, w_in, w_gate_up, b_gate, gla_norm_g, q_norm_g, k_norm_g, w_out, norm_ffn_g,
           peer_wq, peer_keys, peer_u, peer_v):
    batch, t, d = x.shape
    n = batch * t
    bf16 = jnp.bfloat16
    gla_qk_w = GLA_HEADS * GLA_DK
    gla_v_w = GLA_HEADS * GLA_DV
    lr0 = 2 * gla_qk_w + 2 * gla_v_w
    cos_t, sa_t, sb_t = _rope_tables(t, MOBA_HD)

    xcur = x.reshape(n, d)
    for l in range(w_in.shape[0]):
        wt = jnp.swapaxes(w_in[l], 0, 1).astype(bf16)
        wlr_t = jnp.pad(wt[lr0:lr0 + GLA_GATE_RANK], ((0, LANES - GLA_GATE_RANK), (0, 0)))
        wup_pad = jnp.pad(w_gate_up[l], ((0, LANES - GLA_GATE_RANK), (0, 0)))

        proj, glr = _in_proj(xcur, norm_mix_g[l][None, :], wt, wlr_t, lr0, GLA_GATE_RANK)
        o_gla = _gla(proj, glr, wup_pad, b_gate[l][None, :], gla_norm_g[l][None, :], batch, t)
        o_moba = _moba(proj, q_norm_g[l][None, :], k_norm_g[l][None, :], cos_t, sa_t, sb_t, batch, t, lr0)
        x1 = _out_proj(xcur, o_gla, o_moba, w_out[l].astype(bf16))

        xn, st = _peer_q(x1, norm_ffn_g[l][None, :], peer_wq[l].astype(bf16), peer_keys[l])
        s1m, e1, s2m, e2n, tau = _peer_topk(st)
        xcur = _peer_ffn(x1, xn, peer_u[l].astype(bf16), peer_v[l].astype(bf16), s1m, e1, s2m, e2n, tau)
    return xcur.reshape(batch, t, d)
```

```python
import functools
import math

import jax
import jax.numpy as jnp
from jax import lax
from jax.experimental import pallas as pl
from jax.experimental.pallas import tpu as pltpu

GLA_HEADS = 8
GLA_DK = 128
GLA_DV = 256
GLA_GATE_RANK = 16
GLA_GATE_NORMALIZER = 16.0
GLA_CHUNK = 64
MOBA_HEADS = 16
MOBA_HD = 128
MOBA_BLOCK = 256
MOBA_TOPK = 3
ROPE_THETA = 500000.0
ROPE_DIMS = MOBA_HD // 4
PEER_HEADS = 8
PEER_NKEYS = 128
PEER_DQ = 256
PEER_TOPK = 16
EPS = 1e-6

LANES = 128
SUBLANES = 8
VMEM_LIMIT_BYTES = 60 * 1024 * 1024

_HI = lax.Precision.HIGHEST
_NEG_INF = float("-inf")


def _dot(a, b, precision=None):
    return lax.dot_general(a, b, (((1,), (0,)), ((), ())), precision=precision,
                           preferred_element_type=jnp.float32)


def _dot_nt(a, b, precision=None):
    return lax.dot_general(a, b, (((1,), (1,)), ((), ())), precision=precision,
                           preferred_element_type=jnp.float32)


def _dot_tn(a, b, precision=None):
    return lax.dot_general(a, b, (((0,), (0,)), ((), ())), precision=precision,
                           preferred_element_type=jnp.float32)


def _tile(n, pref):
    t = min(n, pref)
    while n % t:
        t //= 2
    return t


def _params(*sem, flags=None):
    return pltpu.CompilerParams(dimension_semantics=sem, vmem_limit_bytes=VMEM_LIMIT_BYTES, flags=flags)


def _in_proj_kernel(x_ref, g_ref, wt_ref, wlrt_ref, o_ref, olr_ref, hb_ref):
    @pl.when(pl.program_id(1) == 0)
    def _():
        x = x_ref[...]
        r = lax.rsqrt(jnp.mean(x * x, axis=-1, keepdims=True) + EPS)
        hb = ((x * r) * g_ref[...]).astype(jnp.bfloat16)
        hb_ref[...] = hb
        olr_ref[...] = _dot_nt(hb, wlrt_ref[...])

    o_ref[...] = _dot_nt(hb_ref[...], wt_ref[...])


def _in_proj(x2, g, wt, wlr_t, split, skip):
    n, d = x2.shape
    cols = wt.shape[0] - skip
    tm = _tile(n, 512)
    tn = _tile(math.gcd(split, cols - split), 1024)
    return pl.pallas_call(
        _in_proj_kernel,
        grid=(n // tm, cols // tn),
        in_specs=[
            pl.BlockSpec((tm, d), lambda i, j: (i, 0)),
            pl.BlockSpec((1, d), lambda i, j: (0, 0)),
            pl.BlockSpec((pl.Element(tn), pl.Element(d)),
                         lambda i, j: (pl.multiple_of(j * tn + jnp.where(j * tn >= split, skip, 0), skip), 0)),
            pl.BlockSpec((LANES, d), lambda i, j: (0, 0)),
        ],
        out_specs=[
            pl.BlockSpec((tm, tn), lambda i, j: (i, j)),
            pl.BlockSpec((tm, LANES), lambda i, j: (i, 0)),
        ],
        out_shape=[
            jax.ShapeDtypeStruct((n, cols), jnp.float32),
            jax.ShapeDtypeStruct((n, LANES), jnp.float32),
        ],
        scratch_shapes=[pltpu.VMEM((tm, d), jnp.bfloat16)],
        compiler_params=_params("parallel", "arbitrary"),
        name="in_proj",
    )(x2, g, wt, wlr_t)


GLA_GROUP = 256


def _gla_kernel(q_ref, k_ref, v_ref, go_ref, glr_ref, wup_ref, bg_ref, ng_ref, o_ref,
                qd_ref, ki_ref, ks_ref, vb_ref, oacc_ref, dec_ref, upd_ref, sprev_ref):
    t, dk = q_ref.shape
    dv = v_ref.shape[1]
    c = GLA_CHUNK
    n = t // c
    grp = min(GLA_GROUP, t)
    scale = dk ** -0.5
    bf16 = jnp.bfloat16

    z = _dot(glr_ref[...], wup_ref[...], _HI) + bg_ref[...]
    log_a = (jnp.minimum(z, 0.0) - jnp.log1p(jnp.exp(-jnp.abs(z)))) / GLA_GATE_NORMALIZER
    row = lax.broadcasted_iota(jnp.int32, (grp, grp), 0)
    col = lax.broadcasted_iota(jnp.int32, (grp, grp), 1)
    same_chunk_causal = (row >= col) & ((row // c) == (col // c))
    tril = same_chunk_causal.astype(jnp.float32)
    bc = jnp.concatenate([_dot(tril, log_a[g * grp:(g + 1) * grp, :], _HI) for g in range(t // grp)], axis=0)
    bc3 = bc.reshape(n, c, dk)
    bl3 = bc3[:, c - 1:c, :]
    k3 = k_ref[...].reshape(n, c, dk)
    qd_ref[...] = (q_ref[...] * scale * jnp.exp(bc)).astype(bf16)
    ki_ref[...] = (k_ref[...] * jnp.exp(-bc)).astype(bf16)
    ks_ref[...] = (k3 * jnp.exp(bl3 - bc3)).reshape(t, dk).astype(bf16)
    dec_ref[...] = jnp.exp(bl3)
    vb_ref[...] = v_ref[...].astype(bf16)

    for g in range(t // grp):
        rows = pl.ds(g * grp, grp)
        att = jnp.where(same_chunk_causal, _dot_nt(qd_ref[rows, :], ki_ref[rows, :]), 0.0)
        oacc_ref[rows, :] = _dot(att.astype(bf16), vb_ref[rows, :])

    def upd_body(ci, carry):
        rows = pl.ds(pl.multiple_of(ci * c, c), c)
        upd_ref[ci] = _dot_tn(vb_ref[rows, :], ks_ref[rows, :])
        return carry

    lax.fori_loop(0, n, upd_body, 0, unroll=4)

    def scan_body(ci, st):
        sprev_ref[ci] = st.astype(bf16)
        return dec_ref[ci] * st + upd_ref[ci]

    lax.fori_loop(0, n, scan_body, jnp.zeros((dv, dk), jnp.float32), unroll=2)

    def inter_body(ci, carry):
        rows = pl.ds(pl.multiple_of(ci * c, c), c)
        oacc_ref[rows, :] += _dot_nt(qd_ref[rows, :], sprev_ref[ci])
        return carry

    lax.fori_loop(0, n, inter_body, 0, unroll=4)

    o = oacc_ref[...]
    r = lax.rsqrt(jnp.mean(o * o, axis=-1, keepdims=True) + EPS)
    go = go_ref[...]
    o_ref[...] = (((o * r) * ng_ref[...]) * (go * jax.nn.sigmoid(go))).astype(o_ref.dtype)


def _gla(proj, glr, wup_pad, b_gate, norm_g, batch, t):
    n = proj.shape[0]
    h, dk, dv = GLA_HEADS, GLA_DK, GLA_DV
    nc = t // GLA_CHUNK
    qk_w = h * dk
    kb = qk_w // dk
    vb = (2 * qk_w) // dv
    gb = (2 * qk_w + h * dv) // dv
    return pl.pallas_call(
        _gla_kernel,
        grid=(batch, h),
        in_specs=[
            pl.BlockSpec((t, dk), lambda b, i: (b, i)),
            pl.BlockSpec((t, dk), lambda b, i: (b, kb + i)),
            pl.BlockSpec((t, dv), lambda b, i: (b, vb + i)),
            pl.BlockSpec((t, dv), lambda b, i: (b, gb + i)),
            pl.BlockSpec((t, LANES), lambda b, i: (b, 0)),
            pl.BlockSpec((LANES, dk), lambda b, i: (0, i)),
            pl.BlockSpec((1, dk), lambda b, i: (0, i)),
            pl.BlockSpec((1, dv), lambda b, i: (0, 0)),
        ],
        out_specs=pl.BlockSpec((t, dv), lambda b, i: (b, i)),
        out_shape=jax.ShapeDtypeStruct((n, h * dv), jnp.bfloat16),
        scratch_shapes=[
            pltpu.VMEM((t, dk), jnp.bfloat16),
            pltpu.VMEM((t, dk), jnp.bfloat16),
            pltpu.VMEM((t, dk), jnp.bfloat16),
            pltpu.VMEM((t, dv), jnp.bfloat16),
            pltpu.VMEM((t, dv), jnp.float32),
            pltpu.VMEM((nc, 1, dk), jnp.float32),
            pltpu.VMEM((nc, dv, dk), jnp.float32),
            pltpu.VMEM((nc, dv, dk), jnp.bfloat16),
        ],
        compiler_params=_params("parallel", "parallel"),
        name="gla",
    )(proj, proj, proj, proj, glr, wup_pad, b_gate, norm_g)


MASK_VALUE = -1e30


def _moba_kernel(q_ref, k_ref, v_ref, qg_ref, kg_ref, cos_ref, sa_ref, sb_ref, o_ref,
                 qa_ref, ka_ref, vb_ref, bias_ref):
    t, hd = q_ref.shape
    blk = MOBA_BLOCK
    nb = t // blk
    half = ROPE_DIMS // 2
    bf16 = jnp.bfloat16

    def prep(x, g, scale):
        r = lax.rsqrt(jnp.mean(x * x, axis=-1, keepdims=True) + EPS)
        xn = (x * r) * g
        y = (xn * cos_ref[...] + pltpu.roll(xn, half, 1) * sa_ref[...]
             + pltpu.roll(xn, hd - half, 1) * sb_ref[...])
        return y * scale

    qh = prep(q_ref[...], qg_ref[...], hd ** -0.5)
    kh = prep(k_ref[...], kg_ref[...], 1.0)
    vb_ref[...] = v_ref[...].astype(bf16)

    km = jnp.sum(kh.reshape(nb, blk, hd), axis=1) * (1.0 / blk)
    gate_t = _dot_nt(km, qh, _HI)
    kblk = lax.broadcasted_iota(jnp.int32, (nb, t), 0)
    qblk = lax.broadcasted_iota(jnp.int32, (nb, t), 1) // blk
    past = kblk < qblk
    gm = jnp.where(past, gate_t, _NEG_INF)
    rank = jnp.zeros((nb, t), jnp.float32)
    for n2 in range(nb):
        other = gm[n2:n2 + 1, :]
        beats = (other > gm) | ((other == gm) & (kblk > n2))
        rank = rank + beats.astype(jnp.float32)
    allowed = (past & (rank < float(MOBA_TOPK))) | (kblk == qblk)
    bias_ref[...] = jnp.zeros_like(bias_ref)
    bias_ref[0:nb, :] = jnp.where(allowed, 0.0, MASK_VALUE)

    qa_ref[:, 0:hd] = qh.astype(bf16)
    qa_ref[:, hd:2 * hd] = bias_ref[...].T.astype(bf16)
    ka_ref[:, 0:hd] = kh.astype(bf16)
    lane = lax.broadcasted_iota(jnp.int32, (t, hd), 1)
    rblk = lax.broadcasted_iota(jnp.int32, (t, hd), 0) // blk
    ka_ref[:, hd:2 * hd] = (lane == rblk).astype(bf16)

    lrow = lax.broadcasted_iota(jnp.int32, (blk, blk), 0)
    lcol = lax.broadcasted_iota(jnp.int32, (blk, blk), 1)
    causal = lrow >= lcol
    for j in range(nb):
        rows = pl.ds(j * blk, blk)
        nkeys = (j + 1) * blk
        s = _dot_nt(qa_ref[rows, :], ka_ref[0:nkeys, :])
        own = jnp.where(causal, s[:, j * blk:nkeys], MASK_VALUE)
        s = own if j == 0 else jnp.concatenate([s[:, :j * blk], own], axis=1)
        m = jnp.max(s, axis=-1, keepdims=True)
        p = jnp.exp(s - m)
        l = jnp.sum(p, axis=-1, keepdims=True)
        o = _dot(p.astype(bf16), vb_ref[0:nkeys, :])
        o_ref[rows, :] = (o * (1.0 / l)).astype(o_ref.dtype)


def _moba(proj, q_g, k_g, cos_t, sa_t, sb_t, batch, t, col0):
    n = proj.shape[0]
    h, hd = MOBA_HEADS, MOBA_HD
    qb = col0 // hd
    kb = qb + h
    vb = kb + h
    tab = pl.BlockSpec((t, hd), lambda b, i: (0, 0))
    return pl.pallas_call(
        _moba_kernel,
        grid=(batch, h),
        in_specs=[
            pl.BlockSpec((t, hd), lambda b, i: (b, qb + i)),
            pl.BlockSpec((t, hd), lambda b, i: (b, kb + i)),
            pl.BlockSpec((t, hd), lambda b, i: (b, vb + i)),
            pl.BlockSpec((1, hd), lambda b, i: (0, 0)),
            pl.BlockSpec((1, hd), lambda b, i: (0, 0)),
            tab, tab, tab,
        ],
        out_specs=pl.BlockSpec((t, hd), lambda b, i: (b, i)),
        out_shape=jax.ShapeDtypeStruct((n, h * hd), jnp.bfloat16),
        scratch_shapes=[
            pltpu.VMEM((t, 2 * hd), jnp.bfloat16),
            pltpu.VMEM((t, 2 * hd), jnp.bfloat16),
            pltpu.VMEM((t, hd), jnp.bfloat16),
            pltpu.VMEM((LANES, t), jnp.float32),
        ],
        compiler_params=_params("parallel", "parallel"),
        name="moba",
    )(proj, proj, proj, q_g, k_g, cos_t, sa_t, sb_t)


def _out_proj_kernel(x_ref, a_ref, b_ref, wa_ref, wb_ref, o_ref):
    o_ref[...] = x_ref[...] + _dot(a_ref[...], wa_ref[...]) + _dot(b_ref[...], wb_ref[...])


def _out_proj(x2, o_gla, o_moba, w):
    n, d = x2.shape
    ka, kb = o_gla.shape[1], o_moba.shape[1]
    assert ka == kb and w.shape[0] == ka + kb
    tm = _tile(n, 512)
    tn = _tile(d, 1024)
    return pl.pallas_call(
        _out_proj_kernel,
        grid=(n // tm, d // tn),
        in_specs=[
            pl.BlockSpec((tm, tn), lambda i, j: (i, j)),
            pl.BlockSpec((tm, ka), lambda i, j: (i, 0)),
            pl.BlockSpec((tm, kb), lambda i, j: (i, 0)),
            pl.BlockSpec((ka, tn), lambda i, j: (0, j)),
            pl.BlockSpec((kb, tn), lambda i, j: (1, j)),
        ],
        out_specs=pl.BlockSpec((tm, tn), lambda i, j: (i, j)),
        out_shape=jax.ShapeDtypeStruct((n, d), jnp.float32),
        compiler_params=_params("parallel", "parallel"),
        name="out_proj",
    )(x2, o_gla, o_moba, w, w)


def _peer_q_kernel(x_ref, g_ref, w_ref, keys_ref, xn_ref, st_ref, hb_ref):
    @pl.when(pl.program_id(1) == 0)
    def _():
        x = x_ref[...]
        r = lax.rsqrt(jnp.mean(x * x, axis=-1, keepdims=True) + EPS)
        hb = ((x * r) * g_ref[...]).astype(jnp.bfloat16)
        hb_ref[...] = hb
        xn_ref[...] = hb

    q = _dot(hb_ref[...], w_ref[...])
    half = keys_ref.shape[3]
    for hh in range(keys_ref.shape[0]):
        for p in range(2):
            c0 = (2 * hh + p) * half
            st_ref[hh, p] = _dot_nt(keys_ref[hh, p], q[:, c0:c0 + half], _HI)


PEER_Q_HEADS_PER_STEP = 2


def _peer_q(x1, g, w_pq, keys):
    n, d = x1.shape
    h, dq, nk = PEER_HEADS, PEER_DQ, PEER_NKEYS
    hp = math.gcd(h, PEER_Q_HEADS_PER_STEP)
    tm = _tile(n, 512)
    return pl.pallas_call(
        _peer_q_kernel,
        grid=(n // tm, h // hp),
        in_specs=[
            pl.BlockSpec((tm, d), lambda i, j: (i, 0)),
            pl.BlockSpec((1, d), lambda i, j: (0, 0)),
            pl.BlockSpec((d, hp * dq), lambda i, j: (0, j)),
            pl.BlockSpec((hp, 2, nk, dq // 2), lambda i, j: (j, 0, 0, 0)),
        ],
        out_specs=[
            pl.BlockSpec((tm, d), lambda i, j: (i, 0)),
            pl.BlockSpec((hp, 2, nk, tm), lambda i, j: (j, 0, 0, i)),
        ],
        out_shape=[
            jax.ShapeDtypeStruct((n, d), jnp.bfloat16),
            jax.ShapeDtypeStruct((h, 2, nk, n), jnp.float32),
        ],
        scratch_shapes=[pltpu.VMEM((tm, d), jnp.bfloat16)],
        compiler_params=_params("parallel", "arbitrary"),
        name="peer_q",
    )(x1, g, w_pq, keys)


def _cmp_exchange(a, i, l):
    hi, lo = jnp.maximum(a[i], a[l]), jnp.minimum(a[i], a[l])
    a[i], a[l] = hi, lo


def _bitonic_merge_desc(a):
    n = len(a)
    j = n // 2
    while j >= 1:
        for i in range(n):
            if i ^ j > i:
                _cmp_exchange(a, i, i ^ j)
        j //= 2


def _top_of_sorted_lists(a):
    k = len(a)
    dist = SUBLANES // 2
    while dist >= 1:
        b = [pltpu.roll(x, SUBLANES - dist, 0) for x in a]
        a = [jnp.maximum(a[i], b[k - 1 - i]) for i in range(k)]
        _bitonic_merge_desc(a)
        dist //= 2
    return a


def _sorted_topk(s, k, out_ref):
    assert s.shape[0] == k * SUBLANES and k & (k - 1) == 0
    a = [s[r * SUBLANES:(r + 1) * SUBLANES, :] for r in range(k)]
    size = 2
    while size <= k:
        j = size // 2
        while j >= 1:
            for i in range(k):
                l = i ^ j
                if l > i:
                    if (i & size) == 0:
                        _cmp_exchange(a, i, l)
                    else:
                        _cmp_exchange(a, l, i)
            j //= 2
        size *= 2
    a = _top_of_sorted_lists(a)
    for r in range(k):
        out_ref[r:r + 1, :] = a[r][0:1, :]
    return a[k - 1][0:1, :]


def _peer_topk_kernel(st_ref, s1m_ref, e1_ref, s2m_ref, e2n_ref, tau_ref, v1_ref, v2_ref):
    k = PEER_TOPK

    def one_head(hh, carry):
        s1 = st_ref[hh, 0]
        s2 = st_ref[hh, 1]
        th1 = _sorted_topk(s1, k, v1_ref)
        th2 = _sorted_topk(s2, k, v2_ref)
        s1m_ref[hh] = jnp.where(s1 >= th1, s1, _NEG_INF)
        s2m_ref[hh] = jnp.where(s2 >= th2, s2, _NEG_INF)
        v1 = v1_ref[...]
        v2 = v2_ref[...]
        m1 = v1[0:1, :]
        m2 = v2[0:1, :]
        e1_ref[hh] = jnp.exp(s1 - m1)
        e2 = jnp.exp(s2 - m2)
        ev1 = jnp.exp(v1 - m1)
        ev2 = jnp.exp(v2 - m2)

        cand = [v1 + v2[b:b + 1, :] for b in range(k)]
        lo = [c[0:SUBLANES, :] for c in cand]
        hi = [c[SUBLANES:2 * SUBLANES, :] for c in cand]
        top = [jnp.maximum(lo[i], hi[k - 1 - i]) for i in range(k)]
        _bitonic_merge_desc(top)
        tau = _top_of_sorted_lists(top)[k - 1][0:1, :]
        zsum = jnp.zeros_like(tau)
        for b in range(k):
            picked = jnp.where(cand[b] >= tau, ev1 * ev2[b:b + 1, :], 0.0)
            zsum = zsum + jnp.sum(picked, axis=0, keepdims=True)
        e2n_ref[hh] = e2 * (1.0 / zsum)
        tau_ref[hh] = tau
        return carry

    lax.fori_loop(0, st_ref.shape[0], one_head, 0)


def _peer_topk(st):
    h, _, nk, n = st.shape
    tl = _tile(n, 256)
    big = pl.BlockSpec((h, nk, tl), lambda i: (0, 0, i))
    shp = jax.ShapeDtypeStruct((h, nk, n), jnp.float32)
    return pl.pallas_call(
        _peer_topk_kernel,
        grid=(n // tl,),
        in_specs=[pl.BlockSpec((h, 2, nk, tl), lambda i: (0, 0, 0, i))],
        out_specs=[big, big, big, big, pl.BlockSpec((h, 1, tl), lambda i: (0, 0, i))],
        out_shape=[shp, shp, shp, shp, jax.ShapeDtypeStruct((h, 1, n), jnp.float32)],
        scratch_shapes=[pltpu.VMEM((PEER_TOPK, tl), jnp.float32),
                        pltpu.VMEM((PEER_TOPK, tl), jnp.float32)],
        compiler_params=_params("parallel"),
        name="peer_topk",
    )(st)


def _peer_ffn_kernel(x1_ref, xn_ref, u_ref, v_ref, s1m_ref, e1_ref, s2m_ref, e2n_ref, tau_ref, o_ref,
                     *, rows_per_step):
    e = pl.program_id(1)
    nh = s2m_ref.shape[0]
    nk = s2m_ref.shape[1]
    rblk = s1m_ref.shape[1]

    @pl.when(e == 0)
    def _():
        o_ref[...] = x1_ref[...]

    tm, d = xn_ref.shape
    base = (e % (rblk // rows_per_step)) * rows_per_step
    slabs = []
    zeros = []
    half_word = jnp.uint32(16)
    for r in range(rows_per_step):
        i8 = base + r
        acc = jnp.zeros((nk, tm), jnp.float32)
        for h in range(nh):
            ssum = s1m_ref[h, pl.ds(i8, 1), :] + s2m_ref[h]
            w = e1_ref[h, pl.ds(i8, 1), :] * e2n_ref[h]
            acc = acc + jnp.where(ssum >= tau_ref[h], w, 0.0)
        slabs.append(acc.T)
        bits = lax.bitcast_convert_type(acc, jnp.uint32)
        zf = lax.bitcast_convert_type(
            lax.shift_right_logical(lax.shift_right_logical(bits, half_word), half_word), jnp.float32)
        for c in range(tm // LANES):
            for g in range(nk // SUBLANES):
                blk = zf[g * SUBLANES:(g + 1) * SUBLANES, c * LANES:(c + 1) * LANES]
                zeros.append(jnp.concatenate([blk, blk], axis=0).astype(jnp.bfloat16))
    wsel = jnp.concatenate(slabs, axis=1) if len(slabs) > 1 else slabs[0]

    rb = 2 * SUBLANES
    kw = 2 * LANES
    n_pairs = (tm // rb) * (d // kw)
    cols = []
    for kb in range(d // kw):
        rows = []
        for mb in range(tm // rb):
            zb = zeros[((kb * (tm // rb) + mb) * len(zeros)) // n_pairs]
            rows.append(xn_ref[mb * rb:(mb + 1) * rb, kb * kw:(kb + 1) * kw] + jnp.concatenate([zb, zb], axis=1))
        cols.append(jnp.concatenate(rows, axis=0))
    hmat = _dot_nt(jnp.concatenate(cols, axis=1), u_ref[...])
    act = 0.5 * hmat * (1.0 + lax.erf(hmat * (1.0 / math.sqrt(2.0))))
    z = jnp.where(wsel != 0.0, wsel * act, 0.0).astype(jnp.bfloat16)
    o_ref[...] += _dot(z, v_ref[...])


def _peer_ffn(x1, xn, u_b, v_b, s1m, e1, s2m, e2n, tau):
    n, d = x1.shape
    ne = u_b.shape[0]
    nh, nk, _ = s1m.shape
    tm = _tile(n, 512)
    te = _tile(ne, 512)
    rows_per_step = te // nk
    rblk = max(SUBLANES, rows_per_step)
    steps_per_rblk = rblk // rows_per_step
    once = dict(pipeline_mode=pl.Buffered(1))
    return pl.pallas_call(
        functools.partial(_peer_ffn_kernel, rows_per_step=rows_per_step),
        grid=(n // tm, ne // te),
        in_specs=[
            pl.BlockSpec((tm, d), lambda i, j: (i, 0), **once),
            pl.BlockSpec((tm, d), lambda i, j: (i, 0)),
            pl.BlockSpec((te, d), lambda i, j: (j, 0)),
            pl.BlockSpec((te, d), lambda i, j: (j, 0)),
            pl.BlockSpec((nh, rblk, tm), lambda i, j: (0, j // steps_per_rblk, i)),
            pl.BlockSpec((nh, rblk, tm), lambda i, j: (0, j // steps_per_rblk, i)),
            pl.BlockSpec((nh, nk, tm), lambda i, j: (0, 0, i)),
            pl.BlockSpec((nh, nk, tm), lambda i, j: (0, 0, i)),
            pl.BlockSpec((nh, 1, tm), lambda i, j: (0, 0, i)),
        ],
        out_specs=pl.BlockSpec((tm, d), lambda i, j: (i, 0)),
        out_shape=jax.ShapeDtypeStruct((n, d), jnp.float32),
        compiler_params=_params("parallel", "arbitrary"),
        name="peer_ffn",
    )(x1, xn, u_b, v_b, s1m, e1, s2m, e2n, tau)


def _rope_tables(t, hd):
    half = ROPE_DIMS // 2
    inv = ROPE_THETA ** (-jnp.arange(half, dtype=jnp.float32) / half)
    ang = jnp.arange(t).astype(jnp.float32)[:, None] * inv[None, :]
    cos, sin = jnp.cos(ang), jnp.sin(ang)
    zeros = jnp.zeros((t, hd - 2 * half), jnp.float32)
    z16 = jnp.zeros((t, half), jnp.float32)
    cos_t = jnp.concatenate([cos, cos, jnp.ones_like(zeros)], axis=1)
    sa_t = jnp.concatenate([z16, sin, zeros], axis=1)
    sb_t = jnp.concatenate([-sin, z16, zeros], axis=1)
    return cos_t, sa_t, sb_t


def kernel(x, norm_mix_g, w_in, w_gate_up, b_gate, gla_norm_g, q_norm_g, k_norm_g, w_out, norm_ffn_g,
           peer_wq, peer_keys, peer_u, peer_v):
    batch, t, d = x.shape
    n = batch * t
    bf16 = jnp.bfloat16
    gla_qk_w = GLA_HEADS * GLA_DK
    gla_v_w = GLA_HEADS * GLA_DV
    lr0 = 2 * gla_qk_w + 2 * gla_v_w
    cos_t, sa_t, sb_t = _rope_tables(t, MOBA_HD)

    xcur = x.reshape(n, d)
    for l in range(w_in.shape[0]):
        wt = jnp.swapaxes(w_in[l], 0, 1).astype(bf16)
        wlr_t = jnp.pad(wt[lr0:lr0 + GLA_GATE_RANK], ((0, LANES - GLA_GATE_RANK), (0, 0)))
        wup_pad = jnp.pad(w_gate_up[l], ((0, LANES - GLA_GATE_RANK), (0, 0)))

        proj, glr = _in_proj(xcur, norm_mix_g[l][None, :], wt, wlr_t, lr0, GLA_GATE_RANK)
        o_gla = _gla(proj, glr, wup_pad, b_gate[l][None, :], gla_norm_g[l][None, :], batch, t)
        o_moba = _moba(proj, q_norm_g[l][None, :], k_norm_g[l][None, :], cos_t, sa_t, sb_t, batch, t, lr0)
        x1 = _out_proj(xcur, o_gla, o_moba, w_out[l].astype(bf16))

        xn, st = _peer_q(x1, norm_ffn_g[l][None, :], peer_wq[l].astype(bf16), peer_keys[l])
        s1m, e1, s2m, e2n, tau = _peer_topk(st)
        xcur = _peer_ffn(x1, xn, peer_u[l].astype(bf16), peer_v[l].astype(bf16), s1m, e1, s2m, e2n, tau)
    return xcur.reshape(batch, t, d)
```

```python
import functools
import math

import jax
import jax.numpy as jnp
from jax import lax
from jax.experimental import pallas as pl
from jax.experimental.pallas import tpu as pltpu

GLA_HEADS = 8
GLA_DK = 128
GLA_DV = 256
GLA_GATE_RANK = 16
GLA_GATE_NORMALIZER = 16.0
GLA_CHUNK = 64
MOBA_HEADS = 16
MOBA_HD = 128
MOBA_BLOCK = 256
MOBA_TOPK = 3
ROPE_THETA = 500000.0
ROPE_DIMS = MOBA_HD // 4
PEER_HEADS = 8
PEER_NKEYS = 128
PEER_DQ = 256
PEER_TOPK = 16
EPS = 1e-6

LANES = 128
SUBLANES = 8
VMEM_LIMIT_BYTES = 60 * 1024 * 1024
_HI = lax.Precision.HIGHEST
_NEG_INF = float("-inf")


def _dot(a, b, precision=None):
    return lax.dot_general(a, b, (((1,), (0,)), ((), ())), precision=precision,
                           preferred_element_type=jnp.float32)


def _dot_nt(a, b, precision=None):
    return lax.dot_general(a, b, (((1,), (1,)), ((), ())), precision=precision,
                           preferred_element_type=jnp.float32)


def _dot_tn(a, b, precision=None):
    return lax.dot_general(a, b, (((0,), (0,)), ((), ())), precision=precision,
                           preferred_element_type=jnp.float32)


def _tile(n, pref):
    t = min(n, pref)
    while n % t:
        t //= 2
    return t


def _params(*sem, flags=None):
    return pltpu.CompilerParams(dimension_semantics=sem, vmem_limit_bytes=VMEM_LIMIT_BYTES, flags=flags)


def _norm_kernel(x_ref, g_ref, wlrt_ref, h_ref, olr_ref):
    x = x_ref[...]
    r = lax.rsqrt(jnp.mean(x * x, axis=-1, keepdims=True) + EPS)
    hb = ((x * r) * g_ref[...]).astype(jnp.bfloat16)
    h_ref[...] = hb
    olr_ref[...] = _dot_nt(hb, wlrt_ref[...])


def _norm_lowrank(x2, g, wlr_t):
    n, d = x2.shape
    tm = _tile(n, 512)
    return pl.pallas_call(
        _norm_kernel,
        grid=(n // tm,),
        in_specs=[
            pl.BlockSpec((tm, d), lambda i: (i, 0)),
            pl.BlockSpec((1, d), lambda i: (0, 0)),
            pl.BlockSpec((LANES, d), lambda i: (0, 0)),
        ],
        out_specs=[
            pl.BlockSpec((tm, d), lambda i: (i, 0)),
            pl.BlockSpec((tm, LANES), lambda i: (i, 0)),
        ],
        out_shape=[
            jax.ShapeDtypeStruct((n, d), jnp.bfloat16),
            jax.ShapeDtypeStruct((n, LANES), jnp.float32),
        ],
        compiler_params=_params("parallel"),
        name="norm_mix",
    )(x2, g, wlr_t)


def _in_proj_kernel(h_ref, wt_ref, o_ref, wb_ref):
    @pl.when(pl.program_id(1) == 0)
    def _():
        wb_ref[...] = wt_ref[...].astype(jnp.bfloat16)

    o_ref[...] = _dot_nt(h_ref[...], wb_ref[...])


def _in_proj(h, wt, split, skip):
    n, d = h.shape
    cols = wt.shape[0] - skip
    tm = _tile(n, 512)
    tn = _tile(math.gcd(split, cols - split), 1024)
    return pl.pallas_call(
        _in_proj_kernel,
        grid=(cols // tn, n // tm),
        in_specs=[
            pl.BlockSpec((tm, d), lambda j, i: (i, 0)),
            pl.BlockSpec((pl.Element(tn), pl.Element(d)),
                         lambda j, i: (pl.multiple_of(j * tn + jnp.where(j * tn >= split, skip, 0), skip), 0)),
        ],
        out_specs=pl.BlockSpec((tm, tn), lambda j, i: (i, j)),
        out_shape=jax.ShapeDtypeStruct((n, cols), jnp.float32),
        scratch_shapes=[pltpu.VMEM((tn, d), jnp.bfloat16)],
        compiler_params=_params("parallel", "arbitrary"),
        name="in_proj",
    )(h, wt)


GLA_GROUP = 256


def _gla_kernel(q_ref, k_ref, v_ref, go_ref, glr_ref, wup_ref, bg_ref, ng_ref, o_ref,
                qd_ref, ki_ref, ks_ref, vb_ref, oacc_ref, dec_ref, upd_ref, sprev_ref):
    t, dk = q_ref.shape
    dv = v_ref.shape[1]
    c = GLA_CHUNK
    n = t // c
    grp = min(GLA_GROUP, t)
    scale = dk ** -0.5
    bf16 = jnp.bfloat16

    z = _dot(glr_ref[...], wup_ref[...], _HI) + bg_ref[...]
    log_a = (jnp.minimum(z, 0.0) - jnp.log1p(jnp.exp(-jnp.abs(z)))) / GLA_GATE_NORMALIZER
    row = lax.broadcasted_iota(jnp.int32, (grp, grp), 0)
    col = lax.broadcasted_iota(jnp.int32, (grp, grp), 1)
    same_chunk_causal = (row >= col) & ((row // c) == (col // c))
    tril = same_chunk_causal.astype(jnp.float32)
    bc = jnp.concatenate([_dot(tril, log_a[g * grp:(g + 1) * grp, :], _HI) for g in range(t // grp)], axis=0)
    bc3 = bc.reshape(n, c, dk)
    bl3 = bc3[:, c - 1:c, :]
    k3 = k_ref[...].reshape(n, c, dk)
    qd_ref[...] = (q_ref[...] * scale * jnp.exp(bc)).astype(bf16)
    ki_ref[...] = (k_ref[...] * jnp.exp(-bc)).astype(bf16)
    ks_ref[...] = (k3 * jnp.exp(bl3 - bc3)).reshape(t, dk).astype(bf16)
    dec_ref[...] = jnp.exp(bl3)
    vb_ref[...] = v_ref[...].astype(bf16)

    for g in range(t // grp):
        rows = pl.ds(g * grp, grp)
        att = jnp.where(same_chunk_causal, _dot_nt(qd_ref[rows, :], ki_ref[rows, :]), 0.0)
        oacc_ref[rows, :] = _dot(att.astype(bf16), vb_ref[rows, :])

    def upd_body(ci, carry):
        rows = pl.ds(pl.multiple_of(ci * c, c), c)
        upd_ref[ci] = _dot_tn(vb_ref[rows, :], ks_ref[rows, :])
        return carry

    lax.fori_loop(0, n, upd_body, 0, unroll=4)

    def scan_body(ci, st):
        sprev_ref[ci] = st.astype(bf16)
        return dec_ref[ci] * st + upd_ref[ci]

    lax.fori_loop(0, n, scan_body, jnp.zeros((dv, dk), jnp.float32), unroll=2)

    def inter_body(ci, carry):
        rows = pl.ds(pl.multiple_of(ci * c, c), c)
        oacc_ref[rows, :] += _dot_nt(qd_ref[rows, :], sprev_ref[ci])
        return carry

    lax.fori_loop(0, n, inter_body, 0, unroll=4)

    o = oacc_ref[...]
    r = lax.rsqrt(jnp.mean(o * o, axis=-1, keepdims=True) + EPS)
    go = go_ref[...]
    o_ref[...] = (((o * r) * ng_ref[...]) * (go * jax.nn.sigmoid(go))).astype(o_ref.dtype)


def _gla(proj, glr, wup_pad, b_gate, norm_g, batch, t):
    n = proj.shape[0]
    h, dk, dv = GLA_HEADS, GLA_DK, GLA_DV
    nc = t // GLA_CHUNK
    qk_w = h * dk
    kb = qk_w // dk
    vb = (2 * qk_w) // dv
    gb = (2 * qk_w + h * dv) // dv
    return pl.pallas_call(
        _gla_kernel,
        grid=(batch, h),
        in_specs=[
            pl.BlockSpec((t, dk), lambda b, i: (b, i)),
            pl.BlockSpec((t, dk), lambda b, i: (b, kb + i)),
            pl.BlockSpec((t, dv), lambda b, i: (b, vb + i)),
            pl.BlockSpec((t, dv), lambda b, i: (b, gb + i)),
            pl.BlockSpec((t, LANES), lambda b, i: (b, 0)),
            pl.BlockSpec((LANES, dk), lambda b, i: (0, i)),
            pl.BlockSpec((1, dk), lambda b, i: (0, i)),
            pl.BlockSpec((1, dv), lambda b, i: (0, 0)),
        ],
        out_specs=pl.BlockSpec((t, dv), lambda b, i: (b, i)),
        out_shape=jax.ShapeDtypeStruct((n, h * dv), jnp.bfloat16),
        scratch_shapes=[
            pltpu.VMEM((t, dk), jnp.bfloat16),
            pltpu.VMEM((t, dk), jnp.bfloat16),
            pltpu.VMEM((t, dk), jnp.bfloat16),
            pltpu.VMEM((t, dv), jnp.bfloat16),
            pltpu.VMEM((t, dv), jnp.float32),
            pltpu.VMEM((nc, 1, dk), jnp.float32),
            pltpu.VMEM((nc, dv, dk), jnp.float32),
            pltpu.VMEM((nc, dv, dk), jnp.bfloat16),
        ],
        compiler_params=_params("parallel", "parallel"),
        name="gla",
    )(proj, proj, proj, proj, glr, wup_pad, b_gate, norm_g)


MASK_VALUE = -1e30


def _moba_kernel(q_ref, k_ref, v_ref, qg_ref, kg_ref, cos_ref, sa_ref, sb_ref, o_ref,
                 qa_ref, ka_ref, vb_ref, bias_ref):
    t, hd = q_ref.shape
    blk = MOBA_BLOCK
    nb = t // blk
    half = ROPE_DIMS // 2
    bf16 = jnp.bfloat16

    def prep(x, g, scale):
        r = lax.rsqrt(jnp.mean(x * x, axis=-1, keepdims=True) + EPS)
        xn = (x * r) * g
        y = (xn * cos_ref[...] + pltpu.roll(xn, half, 1) * sa_ref[...]
             + pltpu.roll(xn, hd - half, 1) * sb_ref[...])
        return y * scale

    qh = prep(q_ref[...], qg_ref[...], hd ** -0.5)
    kh = prep(k_ref[...], kg_ref[...], 1.0)
    vb_ref[...] = v_ref[...].astype(bf16)

    km = jnp.sum(kh.reshape(nb, blk, hd), axis=1) * (1.0 / blk)
    gate_t = _dot_nt(km, qh, _HI)
    kblk = lax.broadcasted_iota(jnp.int32, (nb, t), 0)
    qblk = lax.broadcasted_iota(jnp.int32, (nb, t), 1) // blk
    past = kblk < qblk
    gm = jnp.where(past, gate_t, _NEG_INF)
    rank = jnp.zeros((nb, t), jnp.float32)
    for n2 in range(nb):
        other = gm[n2:n2 + 1, :]
        beats = (other > gm) | ((other == gm) & (kblk > n2))
        rank = rank + beats.astype(jnp.float32)
    allowed = (past & (rank < float(MOBA_TOPK))) | (kblk == qblk)
    bias_ref[...] = jnp.zeros_like(bias_ref)
    bias_ref[0:nb, :] = jnp.where(allowed, 0.0, MASK_VALUE)

    qa_ref[:, 0:hd] = qh.astype(bf16)
    qa_ref[:, hd:2 * hd] = bias_ref[...].T.astype(bf16)
    ka_ref[:, 0:hd] = kh.astype(bf16)
    lane = lax.broadcasted_iota(jnp.int32, (t, hd), 1)
    rblk = lax.broadcasted_iota(jnp.int32, (t, hd), 0) // blk
    ka_ref[:, hd:2 * hd] = (lane == rblk).astype(bf16)

    lrow = lax.broadcasted_iota(jnp.int32, (blk, blk), 0)
    lcol = lax.broadcasted_iota(jnp.int32, (blk, blk), 1)
    causal = lrow >= lcol
    for j in range(nb):
        rows = pl.ds(j * blk, blk)
        nkeys = (j + 1) * blk
        s = _dot_nt(qa_ref[rows, :], ka_ref[0:nkeys, :])
        own = jnp.where(causal, s[:, j * blk:nkeys], MASK_VALUE)
        s = own if j == 0 else jnp.concatenate([s[:, :j * blk], own], axis=1)
        m = jnp.max(s, axis=-1, keepdims=True)
        p = jnp.exp(s - m)
        l = jnp.sum(p, axis=-1, keepdims=True)
        o = _dot(p.astype(bf16), vb_ref[0:nkeys, :])
        o_ref[rows, :] = (o * (1.0 / l)).astype(o_ref.dtype)


def _moba(proj, q_g, k_g, cos_t, sa_t, sb_t, batch, t, col0):
    n = proj.shape[0]
    h, hd = MOBA_HEADS, MOBA_HD
    qb = col0 // hd
    kb = qb + h
    vb = kb + h
    tab = pl.BlockSpec((t, hd), lambda b, i: (0, 0))
    return pl.pallas_call(
        _moba_kernel,
        grid=(batch, h),
        in_specs=[
            pl.BlockSpec((t, hd), lambda b, i: (b, qb + i)),
            pl.BlockSpec((t, hd), lambda b, i: (b, kb + i)),
            pl.BlockSpec((t, hd), lambda b, i: (b, vb + i)),
            pl.BlockSpec((1, hd), lambda b, i: (0, 0)),
            pl.BlockSpec((1, hd), lambda b, i: (0, 0)),
            tab, tab, tab,
        ],
        out_specs=pl.BlockSpec((t, hd), lambda b, i: (b, i)),
        out_shape=jax.ShapeDtypeStruct((n, h * hd), jnp.bfloat16),
        scratch_shapes=[
            pltpu.VMEM((t, 2 * hd), jnp.bfloat16),
            pltpu.VMEM((t, 2 * hd), jnp.bfloat16),
            pltpu.VMEM((t, hd), jnp.bfloat16),
            pltpu.VMEM((LANES, t), jnp.float32),
        ],
        compiler_params=_params("parallel", "parallel"),
        name="moba",
    )(proj, proj, proj, q_g, k_g, cos_t, sa_t, sb_t)


def _out_proj_kernel(x_ref, a_ref, b_ref, wa_ref, wb_ref, o_ref, wab_ref, wbb_ref):
    @pl.when(pl.program_id(1) == 0)
    def _():
        wab_ref[...] = wa_ref[...].astype(jnp.bfloat16)
        wbb_ref[...] = wb_ref[...].astype(jnp.bfloat16)

    o_ref[...] = x_ref[...] + _dot(a_ref[...], wab_ref[...]) + _dot(b_ref[...], wbb_ref[...])


def _out_proj(x2, o_gla, o_moba, w):
    n, d = x2.shape
    ka, kb = o_gla.shape[1], o_moba.shape[1]
    assert ka == kb and w.shape[0] == ka + kb
    tm = _tile(n, 512)
    tn = _tile(d, 512)
    return pl.pallas_call(
        _out_proj_kernel,
        grid=(d // tn, n // tm),
        in_specs=[
            pl.BlockSpec((tm, tn), lambda j, i: (i, j)),
            pl.BlockSpec((tm, ka), lambda j, i: (i, 0)),
            pl.BlockSpec((tm, kb), lambda j, i: (i, 0)),
            pl.BlockSpec((ka, tn), lambda j, i: (0, j)),
            pl.BlockSpec((kb, tn), lambda j, i: (1, j)),
        ],
        out_specs=pl.BlockSpec((tm, tn), lambda j, i: (i, j)),
        out_shape=jax.ShapeDtypeStruct((n, d), jnp.float32),
        scratch_shapes=[pltpu.VMEM((ka, tn), jnp.bfloat16), pltpu.VMEM((kb, tn), jnp.bfloat16)],
        compiler_params=_params("parallel", "arbitrary"),
        name="out_proj",
    )(x2, o_gla, o_moba, w, w)


def _peer_q_kernel(x_ref, g_ref, w_ref, keys_ref, xn_ref, st_ref, hb_ref):
    @pl.when(pl.program_id(1) == 0)
    def _():
        x = x_ref[...]
        r = lax.rsqrt(jnp.mean(x * x, axis=-1, keepdims=True) + EPS)
        hb = ((x * r) * g_ref[...]).astype(jnp.bfloat16)
        hb_ref[...] = hb
        xn_ref[...] = hb

    q = _dot(hb_ref[...], w_ref[...])
    half = keys_ref.shape[3]
    for hh in range(keys_ref.shape[0]):
        for p in range(2):
            c0 = (2 * hh + p) * half
            st_ref[hh, p] = _dot_nt(keys_ref[hh, p], q[:, c0:c0 + half], _HI)


PEER_Q_HEADS_PER_STEP = 2


def _peer_q(x1, g, w_pq, keys):
    n, d = x1.shape
    h, dq, nk = PEER_HEADS, PEER_DQ, PEER_NKEYS
    hp = math.gcd(h, PEER_Q_HEADS_PER_STEP)
    tm = _tile(n, 512)
    return pl.pallas_call(
        _peer_q_kernel,
        grid=(n // tm, h // hp),
        in_specs=[
            pl.BlockSpec((tm, d), lambda i, j: (i, 0)),
            pl.BlockSpec((1, d), lambda i, j: (0, 0)),
            pl.BlockSpec((d, hp * dq), lambda i, j: (0, j)),
            pl.BlockSpec((hp, 2, nk, dq // 2), lambda i, j: (j, 0, 0, 0)),
        ],
        out_specs=[
            pl.BlockSpec((tm, d), lambda i, j: (i, 0)),
            pl.BlockSpec((hp, 2, nk, tm), lambda i, j: (j, 0, 0, i)),
        ],
        out_shape=[
            jax.ShapeDtypeStruct((n, d), jnp.bfloat16),
            jax.ShapeDtypeStruct((h, 2, nk, n), jnp.float32),
        ],
        scratch_shapes=[pltpu.VMEM((tm, d), jnp.bfloat16)],
        compiler_params=_params("parallel", "arbitrary"),
        name="peer_q",
    )(x1, g, w_pq, keys)


def _cmp_exchange(a, i, l):
    hi, lo = jnp.maximum(a[i], a[l]), jnp.minimum(a[i], a[l])
    a[i], a[l] = hi, lo


def _bitonic_merge_desc(a):
    n = len(a)
    j = n // 2
    while j >= 1:
        for i in range(n):
            if i ^ j > i:
                _cmp_exchange(a, i, i ^ j)
        j //= 2


def _top_of_sorted_lists(a):
    k = len(a)
    dist = SUBLANES // 2
    while dist >= 1:
        b = [pltpu.roll(x, SUBLANES - dist, 0) for x in a]
        a = [jnp.maximum(a[i], b[k - 1 - i]) for i in range(k)]
        _bitonic_merge_desc(a)
        dist //= 2
    return a


def _sorted_topk(s, k, out_ref):
    assert s.shape[0] == k * SUBLANES and k & (k - 1) == 0
    a = [s[r * SUBLANES:(r + 1) * SUBLANES, :] for r in range(k)]
    size = 2
    while size <= k:
        j = size // 2
        while j >= 1:
            for i in range(k):
                l = i ^ j
                if l > i:
                    if (i & size) == 0:
                        _cmp_exchange(a, i, l)
                    else:
                        _cmp_exchange(a, l, i)
            j //= 2
        size *= 2
    a = _top_of_sorted_lists(a)
    for r in range(k):
        out_ref[r:r + 1, :] = a[r][0:1, :]
    return a[k - 1][0:1, :]


def _peer_topk_kernel(st_ref, s1m_ref, e1_ref, s2m_ref, e2n_ref, tau_ref, v1_ref, v2_ref):
    k = PEER_TOPK

    def one_head(hh, carry):
        s1 = st_ref[hh, 0]
        s2 = st_ref[hh, 1]
        th1 = _sorted_topk(s1, k, v1_ref)
        th2 = _sorted_topk(s2, k, v2_ref)
        s1m_ref[hh] = jnp.where(s1 >= th1, s1, _NEG_INF)
        s2m_ref[hh] = jnp.where(s2 >= th2, s2, _NEG_INF)
        v1 = v1_ref[...]
        v2 = v2_ref[...]
        m1 = v1[0:1, :]
        m2 = v2[0:1, :]
        e1_ref[hh] = jnp.exp(s1 - m1)
        e2 = jnp.exp(s2 - m2)
        ev1 = jnp.exp(v1 - m1)
        ev2 = jnp.exp(v2 - m2)

        cand = [v1 + v2[b:b + 1, :] for b in range(k)]
        lo = [c[0:SUBLANES, :] for c in cand]
        hi = [c[SUBLANES:2 * SUBLANES, :] for c in cand]
        top = [jnp.maximum(lo[i], hi[k - 1 - i]) for i in range(k)]
        _bitonic_merge_desc(top)
        tau = _top_of_sorted_lists(top)[k - 1][0:1, :]
        zsum = jnp.zeros_like(tau)
        for b in range(k):
            picked = jnp.where(cand[b] >= tau, ev1 * ev2[b:b + 1, :], 0.0)
            zsum = zsum + jnp.sum(picked, axis=0, keepdims=True)
        e2n_ref[hh] = e2 * (1.0 / zsum)
        tau_ref[hh] = tau
        return carry

    lax.fori_loop(0, st_ref.shape[0], one_head, 0)


def _peer_topk(st):
    h, _, nk, n = st.shape
    tl = _tile(n, 256)
    big = pl.BlockSpec((h, nk, tl), lambda i: (0, 0, i))
    shp = jax.ShapeDtypeStruct((h, nk, n), jnp.float32)
    return pl.pallas_call(
        _peer_topk_kernel,
        grid=(n // tl,),
        in_specs=[pl.BlockSpec((h, 2, nk, tl), lambda i: (0, 0, 0, i))],
        out_specs=[big, big, big, big, pl.BlockSpec((h, 1, tl), lambda i: (0, 0, i))],
        out_shape=[shp, shp, shp, shp, jax.ShapeDtypeStruct((h, 1, n), jnp.float32)],
        scratch_shapes=[pltpu.VMEM((PEER_TOPK, tl), jnp.float32),
                        pltpu.VMEM((PEER_TOPK, tl), jnp.float32)],
        compiler_params=_params("parallel"),
        name="peer_topk",
    )(st)


def _peer_ffn_kernel(x1_ref, xn_ref, u_ref, v_ref, s1m_ref, e1_ref, s2m_ref, e2n_ref, tau_ref, o_ref,
                     *, rows_per_step):
    e = pl.program_id(1)
    nh = s2m_ref.shape[0]
    nk = s2m_ref.shape[1]
    rblk = s1m_ref.shape[1]

    @pl.when(e == 0)
    def _():
        o_ref[...] = x1_ref[...]

    tm, d = xn_ref.shape
    base = (e % (rblk // rows_per_step)) * rows_per_step
    slabs = []
    zeros = []
    half_word = jnp.uint32(16)
    for r in range(rows_per_step):
        i8 = base + r
        acc = jnp.zeros((nk, tm), jnp.float32)
        for h in range(nh):
            ssum = s1m_ref[h, pl.ds(i8, 1), :] + s2m_ref[h]
            w = e1_ref[h, pl.ds(i8, 1), :] * e2n_ref[h]
            acc = acc + jnp.where(ssum >= tau_ref[h], w, 0.0)
        slabs.append(acc.T)
        bits = lax.bitcast_convert_type(acc, jnp.uint32)
        zf = lax.bitcast_convert_type(
            lax.shift_right_logical(lax.shift_right_logical(bits, half_word), half_word), jnp.float32)
        for c in range(tm // LANES):
            for g in range(nk // SUBLANES):
                blk = zf[g * SUBLANES:(g + 1) * SUBLANES, c * LANES:(c + 1) * LANES]
                zeros.append(jnp.concatenate([blk, blk], axis=0).astype(jnp.bfloat16))
    wsel = jnp.concatenate(slabs, axis=1) if len(slabs) > 1 else slabs[0]

    rb = 2 * SUBLANES
    kw = 2 * LANES
    n_pairs = (tm // rb) * (d // kw)
    cols = []
    for kb in range(d // kw):
        rows = []
        for mb in range(tm // rb):
            zb = zeros[((kb * (tm // rb) + mb) * len(zeros)) // n_pairs]
            rows.append(xn_ref[mb * rb:(mb + 1) * rb, kb * kw:(kb + 1) * kw] + jnp.concatenate([zb, zb], axis=1))
        cols.append(jnp.concatenate(rows, axis=0))
    hmat = _dot_nt(jnp.concatenate(cols, axis=1), u_ref[...])
    act = 0.5 * hmat * (1.0 + lax.erf(hmat * (1.0 / math.sqrt(2.0))))
    z = jnp.where(wsel != 0.0, wsel * act, 0.0).astype(jnp.bfloat16)
    o_ref[...] += _dot(z, v_ref[...])


def _peer_ffn(x1, xn, u_b, v_b, s1m, e1, s2m, e2n, tau):
    n, d = x1.shape
    ne = u_b.shape[0]
    nh, nk, _ = s1m.shape
    tm = _tile(n, 512)
    te = _tile(ne, 512)
    rows_per_step = te // nk
    rblk = max(SUBLANES, rows_per_step)
    steps_per_rblk = rblk // rows_per_step
    once = dict(pipeline_mode=pl.Buffered(1))
    return pl.pallas_call(
        functools.partial(_peer_ffn_kernel, rows_per_step=rows_per_step),
        grid=(n // tm, ne // te),
        in_specs=[
            pl.BlockSpec((tm, d), lambda i, j: (i, 0), **once),
            pl.BlockSpec((tm, d), lambda i, j: (i, 0)),
            pl.BlockSpec((te, d), lambda i, j: (j, 0)),
            pl.BlockSpec((te, d), lambda i, j: (j, 0)),
            pl.BlockSpec((nh, rblk, tm), lambda i, j: (0, j // steps_per_rblk, i)),
            pl.BlockSpec((nh, rblk, tm), lambda i, j: (0, j // steps_per_rblk, i)),
            pl.BlockSpec((nh, nk, tm), lambda i, j: (0, 0, i)),
            pl.BlockSpec((nh, nk, tm), lambda i, j: (0, 0, i)),
            pl.BlockSpec((nh, 1, tm), lambda i, j: (0, 0, i)),
        ],
        out_specs=pl.BlockSpec((tm, d), lambda i, j: (i, 0)),
        out_shape=jax.ShapeDtypeStruct((n, d), jnp.float32),
        compiler_params=_params("parallel", "arbitrary"),
        name="peer_ffn",
    )(x1, xn, u_b, v_b, s1m, e1, s2m, e2n, tau)


def _rope_tables(t, hd):
    half = ROPE_DIMS // 2
    inv = ROPE_THETA ** (-jnp.arange(half, dtype=jnp.float32) / half)
    ang = jnp.arange(t).astype(jnp.float32)[:, None] * inv[None, :]
    cos, sin = jnp.cos(ang), jnp.sin(ang)
    zeros = jnp.zeros((t, hd - 2 * half), jnp.float32)
    z16 = jnp.zeros((t, half), jnp.float32)
    cos_t = jnp.concatenate([cos, cos, jnp.ones_like(zeros)], axis=1)
    sa_t = jnp.concatenate([z16, sin, zeros], axis=1)
    sb_t = jnp.concatenate([-sin, z16, zeros], axis=1)
    return cos_t, sa_t, sb_t


def kernel(x, norm_mix_g, w_in, w_gate_up, b_gate, gla_norm_g, q_norm_g, k_norm_g, w_out, norm_ffn_g,
           peer_wq, peer_keys, peer_u, peer_v):
    batch, t, d = x.shape
    n = batch * t
    bf16 = jnp.bfloat16
    gla_qk_w = GLA_HEADS * GLA_DK
    gla_v_w = GLA_HEADS * GLA_DV
    lr0 = 2 * gla_qk_w + 2 * gla_v_w
    cos_t, sa_t, sb_t = _rope_tables(t, MOBA_HD)

    xcur = x.reshape(n, d)
    for l in range(w_in.shape[0]):
        wt = jnp.swapaxes(w_in[l], 0, 1)
        wlr_t = jnp.pad(wt[lr0:lr0 + GLA_GATE_RANK], ((0, LANES - GLA_GATE_RANK), (0, 0))).astype(bf16)
        wup_pad = jnp.pad(w_gate_up[l], ((0, LANES - GLA_GATE_RANK), (0, 0)))

        h, glr = _norm_lowrank(xcur, norm_mix_g[l][None, :], wlr_t)
        proj = _in_proj(h, wt, lr0, GLA_GATE_RANK)
        o_gla = _gla(proj, glr, wup_pad, b_gate[l][None, :], gla_norm_g[l][None, :], batch, t)
        o_moba = _moba(proj, q_norm_g[l][None, :], k_norm_g[l][None, :], cos_t, sa_t, sb_t, batch, t, lr0)
        x1 = _out_proj(xcur, o_gla, o_moba, w_out[l])

        xn, st = _peer_q(x1, norm_ffn_g[l][None, :], peer_wq[l].astype(bf16), peer_keys[l])
        s1m, e1, s2m, e2n, tau = _peer_topk(st)
        xcur = _peer_ffn(x1, xn, peer_u[l].astype(bf16), peer_v[l].astype(bf16), s1m, e1, s2m, e2n, tau)
    return xcur.reshape(batch, t, d)
```

```python
import functools
import math

import jax
import jax.numpy as jnp
from jax import lax
from jax.experimental import pallas as pl
from jax.experimental.pallas import tpu as pltpu

GLA_HEADS = 8
GLA_DK = 128
GLA_DV = 256
GLA_GATE_RANK = 16
GLA_GATE_NORMALIZER = 16.0
GLA_CHUNK = 64
MOBA_HEADS = 16
MOBA_HD = 128
MOBA_BLOCK = 256
MOBA_TOPK = 3
ROPE_THETA = 500000.0
ROPE_DIMS = MOBA_HD // 4
PEER_HEADS = 8
PEER_NKEYS = 128
PEER_DQ = 256
PEER_TOPK = 16
EPS = 1e-6

LANES = 128
SUBLANES = 8
VMEM_LIMIT_BYTES = 60 * 1024 * 1024
_HI = lax.Precision.HIGHEST
_NEG_INF = float("-inf")


def _dot(a, b, precision=None):
    return lax.dot_general(a, b, (((1,), (0,)), ((), ())), precision=precision,
                           preferred_element_type=jnp.float32)


def _dot_nt(a, b, precision=None):
    return lax.dot_general(a, b, (((1,), (1,)), ((), ())), precision=precision,
                           preferred_element_type=jnp.float32)


def _dot_tn(a, b, precision=None):
    return lax.dot_general(a, b, (((0,), (0,)), ((), ())), precision=precision,
                           preferred_element_type=jnp.float32)


def _tile(n, pref):
    t = min(n, pref)
    while n % t:
        t //= 2
    return t


def _params(*sem, flags=None):
    return pltpu.CompilerParams(dimension_semantics=sem, vmem_limit_bytes=VMEM_LIMIT_BYTES, flags=flags)


def _norm_kernel(x_ref, g_ref, wlrt_ref, h_ref, olr_ref):
    x = x_ref[...]
    r = lax.rsqrt(jnp.mean(x * x, axis=-1, keepdims=True) + EPS)
    hb = ((x * r) * g_ref[...]).astype(jnp.bfloat16)
    h_ref[...] = hb
    olr_ref[...] = _dot_nt(hb, wlrt_ref[...])


def _norm_lowrank(x2, g, wlr_t):
    n, d = x2.shape
    tm = _tile(n, 512)
    return pl.pallas_call(
        _norm_kernel,
        grid=(n // tm,),
        in_specs=[
            pl.BlockSpec((tm, d), lambda i: (i, 0)),
            pl.BlockSpec((1, d), lambda i: (0, 0)),
            pl.BlockSpec((LANES, d), lambda i: (0, 0)),
        ],
        out_specs=[
            pl.BlockSpec((tm, d), lambda i: (i, 0)),
            pl.BlockSpec((tm, LANES), lambda i: (i, 0)),
        ],
        out_shape=[
            jax.ShapeDtypeStruct((n, d), jnp.bfloat16),
            jax.ShapeDtypeStruct((n, LANES), jnp.float32),
        ],
        compiler_params=_params("parallel"),
        name="norm_mix",
    )(x2, g, wlr_t)


def _in_proj_kernel(h_ref, wt_ref, o_ref, wb_ref):
    @pl.when(pl.program_id(1) == 0)
    def _():
        wb_ref[...] = wt_ref[...].astype(jnp.bfloat16)

    o_ref[...] = _dot_nt(h_ref[...], wb_ref[...])


def _in_proj(h, wt, split, skip):
    n, d = h.shape
    cols = wt.shape[0] - skip
    tm = _tile(n, 512)
    tn = _tile(math.gcd(split, cols - split), 1024)
    return pl.pallas_call(
        _in_proj_kernel,
        grid=(cols // tn, n // tm),
        in_specs=[
            pl.BlockSpec((tm, d), lambda j, i: (i, 0)),
            pl.BlockSpec((pl.Element(tn), pl.Element(d)),
                         lambda j, i: (pl.multiple_of(j * tn + jnp.where(j * tn >= split, skip, 0), skip), 0)),
        ],
        out_specs=pl.BlockSpec((tm, tn), lambda j, i: (i, j)),
        out_shape=jax.ShapeDtypeStruct((n, cols), jnp.float32),
        scratch_shapes=[pltpu.VMEM((tn, d), jnp.bfloat16)],
        compiler_params=_params("parallel", "arbitrary"),
        name="in_proj",
    )(h, wt)


GLA_GROUP = 256


def _gla_kernel(q_ref, k_ref, v_ref, go_ref, glr_ref, wup_ref, bg_ref, ng_ref, o_ref,
                qd_ref, ki_ref, ks_ref, vb_ref, oacc_ref, dec_ref, upd_ref, sprev_ref):
    t, dk = q_ref.shape
    dv = v_ref.shape[1]
    c = GLA_CHUNK
    n = t // c
    grp = min(GLA_GROUP, t)
    scale = dk ** -0.5
    bf16 = jnp.bfloat16

    z = _dot(glr_ref[...], wup_ref[...], _HI) + bg_ref[...]
    log_a = (jnp.minimum(z, 0.0) - jnp.log1p(jnp.exp(-jnp.abs(z)))) / GLA_GATE_NORMALIZER
    row = lax.broadcasted_iota(jnp.int32, (grp, grp), 0)
    col = lax.broadcasted_iota(jnp.int32, (grp, grp), 1)
    same_chunk_causal = (row >= col) & ((row // c) == (col // c))
    tril = same_chunk_causal.astype(jnp.float32)
    bc = jnp.concatenate([_dot(tril, log_a[g * grp:(g + 1) * grp, :], _HI) for g in range(t // grp)], axis=0)
    bc3 = bc.reshape(n, c, dk)
    bl3 = bc3[:, c - 1:c, :]
    k3 = k_ref[...].reshape(n, c, dk)
    qd_ref[...] = (q_ref[...] * scale * jnp.exp(bc)).astype(bf16)
    ki_ref[...] = (k_ref[...] * jnp.exp(-bc)).astype(bf16)
    ks_ref[...] = (k3 * jnp.exp(bl3 - bc3)).reshape(t, dk).astype(bf16)
    dec_ref[...] = jnp.exp(bl3)
    vb_ref[...] = v_ref[...].astype(bf16)

    for g in range(t // grp):
        rows = pl.ds(g * grp, grp)
        att = jnp.where(same_chunk_causal, _dot_nt(qd_ref[rows, :], ki_ref[rows, :]), 0.0)
        oacc_ref[rows, :] = _dot(att.astype(bf16), vb_ref[rows, :])

    def upd_body(ci, carry):
        rows = pl.ds(pl.multiple_of(ci * c, c), c)
        upd_ref[ci] = _dot_tn(vb_ref[rows, :], ks_ref[rows, :])
        return carry

    lax.fori_loop(0, n, upd_body, 0, unroll=4)

    def scan_body(ci, st):
        sprev_ref[ci] = st.astype(bf16)
        return dec_ref[ci] * st + upd_ref[ci]

    lax.fori_loop(0, n, scan_body, jnp.zeros((dv, dk), jnp.float32), unroll=2)

    def inter_body(ci, carry):
        rows = pl.ds(pl.multiple_of(ci * c, c), c)
        oacc_ref[rows, :] += _dot_nt(qd_ref[rows, :], sprev_ref[ci])
        return carry

    lax.fori_loop(0, n, inter_body, 0, unroll=4)

    o = oacc_ref[...]
    r = lax.rsqrt(jnp.mean(o * o, axis=-1, keepdims=True) + EPS)
    go = go_ref[...]
    o_ref[...] = (((o * r) * ng_ref[...]) * (go * jax.nn.sigmoid(go))).astype(o_ref.dtype)


def _gla(proj, glr, wup_pad, b_gate, norm_g, batch, t):
    n = proj.shape[0]
    h, dk, dv = GLA_HEADS, GLA_DK, GLA_DV
    nc = t // GLA_CHUNK
    qk_w = h * dk
    kb = qk_w // dk
    vb = (2 * qk_w) // dv
    gb = (2 * qk_w + h * dv) // dv
    return pl.pallas_call(
        _gla_kernel,
        grid=(batch, h),
        in_specs=[
            pl.BlockSpec((t, dk), lambda b, i: (b, i)),
            pl.BlockSpec((t, dk), lambda b, i: (b, kb + i)),
            pl.BlockSpec((t, dv), lambda b, i: (b, vb + i)),
            pl.BlockSpec((t, dv), lambda b, i: (b, gb + i)),
            pl.BlockSpec((t, LANES), lambda b, i: (b, 0)),
            pl.BlockSpec((LANES, dk), lambda b, i: (0, i)),
            pl.BlockSpec((1, dk), lambda b, i: (0, i)),
            pl.BlockSpec((1, dv), lambda b, i: (0, 0)),
        ],
        out_specs=pl.BlockSpec((t, dv), lambda b, i: (b, i)),
        out_shape=jax.ShapeDtypeStruct((n, h * dv), jnp.bfloat16),
        scratch_shapes=[
            pltpu.VMEM((t, dk), jnp.bfloat16),
            pltpu.VMEM((t, dk), jnp.bfloat16),
            pltpu.VMEM((t, dk), jnp.bfloat16),
            pltpu.VMEM((t, dv), jnp.bfloat16),
            pltpu.VMEM((t, dv), jnp.float32),
            pltpu.VMEM((nc, 1, dk), jnp.float32),
            pltpu.VMEM((nc, dv, dk), jnp.float32),
            pltpu.VMEM((nc, dv, dk), jnp.bfloat16),
        ],
        compiler_params=_params("parallel", "parallel"),
        name="gla",
    )(proj, proj, proj, proj, glr, wup_pad, b_gate, norm_g)


MASK_VALUE = -1e30


def _moba_kernel(q_ref, k_ref, v_ref, qg_ref, kg_ref, cos_ref, sa_ref, sb_ref, ta_ref, tb_ref,
                 o_ref, tab_ref, tbb_ref, qa_ref, ka_ref, vb_ref, bias_ref):
    t, hd = q_ref.shape
    blk = MOBA_BLOCK
    nb = t // blk
    half = ROPE_DIMS // 2
    bf16 = jnp.bfloat16

    tab_ref[...] = ta_ref[...].astype(bf16)
    tbb_ref[...] = tb_ref[...].astype(bf16)

    def prep(x, g, scale):
        r = lax.rsqrt(jnp.mean(x * x, axis=-1, keepdims=True) + EPS)
        xn = (x * r) * g
        y = (xn * cos_ref[...] + pltpu.roll(xn, half, 1) * sa_ref[...]
             + pltpu.roll(xn, hd - half, 1) * sb_ref[...])
        return y * scale

    qh = prep(q_ref[...], qg_ref[...], hd ** -0.5)
    kh = prep(k_ref[...], kg_ref[...], 1.0)
    vb_ref[...] = v_ref[...].astype(bf16)

    km = jnp.sum(kh.reshape(nb, blk, hd), axis=1) * (1.0 / blk)
    gate_t = _dot_nt(km, qh, _HI)
    kblk = lax.broadcasted_iota(jnp.int32, (nb, t), 0)
    qblk = lax.broadcasted_iota(jnp.int32, (nb, t), 1) // blk
    past = kblk < qblk
    gm = jnp.where(past, gate_t, _NEG_INF)
    rank = jnp.zeros((nb, t), jnp.float32)
    for n2 in range(nb):
        other = gm[n2:n2 + 1, :]
        beats = (other > gm) | ((other == gm) & (kblk > n2))
        rank = rank + beats.astype(jnp.float32)
    allowed = (past & (rank < float(MOBA_TOPK))) | (kblk == qblk)
    bias_ref[...] = jnp.zeros_like(bias_ref)
    bias_ref[0:nb, :] = jnp.where(allowed, 0.0, MASK_VALUE)

    qa_ref[:, 0:hd] = qh.astype(bf16)
    qa_ref[:, hd:2 * hd] = bias_ref[...].T.astype(bf16)
    ka_ref[:, 0:hd] = kh.astype(bf16)
    lane = lax.broadcasted_iota(jnp.int32, (t, hd), 1)
    rblk = lax.broadcasted_iota(jnp.int32, (t, hd), 0) // blk
    ka_ref[:, hd:2 * hd] = (lane == rblk).astype(bf16)

    lrow = lax.broadcasted_iota(jnp.int32, (blk, blk), 0)
    lcol = lax.broadcasted_iota(jnp.int32, (blk, blk), 1)
    causal = lrow >= lcol
    for j in range(nb):
        rows = pl.ds(j * blk, blk)
        nkeys = (j + 1) * blk
        s = _dot_nt(qa_ref[rows, :], ka_ref[0:nkeys, :])
        own = jnp.where(causal, s[:, j * blk:nkeys], MASK_VALUE)
        s = own if j == 0 else jnp.concatenate([s[:, :j * blk], own], axis=1)
        m = jnp.max(s, axis=-1, keepdims=True)
        p = jnp.exp(s - m)
        l = jnp.sum(p, axis=-1, keepdims=True)
        o = _dot(p.astype(bf16), vb_ref[0:nkeys, :])
        o_ref[rows, :] = (o * (1.0 / l)).astype(o_ref.dtype)


def _moba(proj, q_g, k_g, cos_t, sa_t, sb_t, batch, t, col0, table_a, table_b):
    n = proj.shape[0]
    h, hd = MOBA_HEADS, MOBA_HD
    qb = col0 // hd
    kb = qb + h
    vb = kb + h
    tab = pl.BlockSpec((t, hd), lambda b, i: (0, 0))
    trows, tcols = table_a.shape
    assert table_b.shape == table_a.shape and trows % (batch * h) == 0
    slab = pl.BlockSpec((trows // (batch * h), tcols), lambda b, i: (b * h + i, 0))
    return pl.pallas_call(
        _moba_kernel,
        grid=(batch, h),
        in_specs=[
            pl.BlockSpec((t, hd), lambda b, i: (b, qb + i)),
            pl.BlockSpec((t, hd), lambda b, i: (b, kb + i)),
            pl.BlockSpec((t, hd), lambda b, i: (b, vb + i)),
            pl.BlockSpec((1, hd), lambda b, i: (0, 0)),
            pl.BlockSpec((1, hd), lambda b, i: (0, 0)),
            tab, tab, tab, slab, slab,
        ],
        out_specs=[pl.BlockSpec((t, hd), lambda b, i: (b, i)), slab, slab],
        out_shape=[jax.ShapeDtypeStruct((n, h * hd), jnp.bfloat16),
                   jax.ShapeDtypeStruct(table_a.shape, jnp.bfloat16),
                   jax.ShapeDtypeStruct(table_b.shape, jnp.bfloat16)],
        scratch_shapes=[
            pltpu.VMEM((t, 2 * hd), jnp.bfloat16),
            pltpu.VMEM((t, 2 * hd), jnp.bfloat16),
            pltpu.VMEM((t, hd), jnp.bfloat16),
            pltpu.VMEM((LANES, t), jnp.float32),
        ],
        compiler_params=_params("parallel", "parallel"),
        name="moba",
    )(proj, proj, proj, q_g, k_g, cos_t, sa_t, sb_t, table_a, table_b)


def _out_proj_kernel(x_ref, a_ref, b_ref, wa_ref, wb_ref, o_ref, wab_ref, wbb_ref):
    @pl.when(pl.program_id(1) == 0)
    def _():
        wab_ref[...] = wa_ref[...].astype(jnp.bfloat16)
        wbb_ref[...] = wb_ref[...].astype(jnp.bfloat16)

    o_ref[...] = x_ref[...] + _dot(a_ref[...], wab_ref[...]) + _dot(b_ref[...], wbb_ref[...])


def _out_proj(x2, o_gla, o_moba, w):
    n, d = x2.shape
    ka, kb = o_gla.shape[1], o_moba.shape[1]
    assert ka == kb and w.shape[0] == ka + kb
    tm = _tile(n, 512)
    tn = _tile(d, 1024)
    return pl.pallas_call(
        _out_proj_kernel,
        grid=(d // tn, n // tm),
        in_specs=[
            pl.BlockSpec((tm, tn), lambda j, i: (i, j)),
            pl.BlockSpec((tm, ka), lambda j, i: (i, 0)),
            pl.BlockSpec((tm, kb), lambda j, i: (i, 0)),
            pl.BlockSpec((ka, tn), lambda j, i: (0, j)),
            pl.BlockSpec((kb, tn), lambda j, i: (1, j)),
        ],
        out_specs=pl.BlockSpec((tm, tn), lambda j, i: (i, j)),
        out_shape=jax.ShapeDtypeStruct((n, d), jnp.float32),
        scratch_shapes=[pltpu.VMEM((ka, tn), jnp.bfloat16), pltpu.VMEM((kb, tn), jnp.bfloat16)],
        compiler_params=_params("parallel", "arbitrary"),
        name="out_proj",
    )(x2, o_gla, o_moba, w, w)


def _peer_q_kernel(x_ref, g_ref, w_ref, keys_ref, xn_ref, st_ref, hb_ref):
    @pl.when(pl.program_id(1) == 0)
    def _():
        x = x_ref[...]
        r = lax.rsqrt(jnp.mean(x * x, axis=-1, keepdims=True) + EPS)
        hb = ((x * r) * g_ref[...]).astype(jnp.bfloat16)
        hb_ref[...] = hb
        xn_ref[...] = hb

    q = _dot(hb_ref[...], w_ref[...])
    half = keys_ref.shape[3]
    for hh in range(keys_ref.shape[0]):
        for p in range(2):
            c0 = (2 * hh + p) * half
            st_ref[hh, p] = _dot_nt(keys_ref[hh, p], q[:, c0:c0 + half], _HI)


PEER_Q_HEADS_PER_STEP = 2


def _peer_q(x1, g, w_pq, keys):
    n, d = x1.shape
    h, dq, nk = PEER_HEADS, PEER_DQ, PEER_NKEYS
    hp = math.gcd(h, PEER_Q_HEADS_PER_STEP)
    tm = _tile(n, 512)
    return pl.pallas_call(
        _peer_q_kernel,
        grid=(n // tm, h // hp),
        in_specs=[
            pl.BlockSpec((tm, d), lambda i, j: (i, 0)),
            pl.BlockSpec((1, d), lambda i, j: (0, 0)),
            pl.BlockSpec((d, hp * dq), lambda i, j: (0, j)),
            pl.BlockSpec((hp, 2, nk, dq // 2), lambda i, j: (j, 0, 0, 0)),
        ],
        out_specs=[
            pl.BlockSpec((tm, d), lambda i, j: (i, 0)),
            pl.BlockSpec((hp, 2, nk, tm), lambda i, j: (j, 0, 0, i)),
        ],
        out_shape=[
            jax.ShapeDtypeStruct((n, d), jnp.bfloat16),
            jax.ShapeDtypeStruct((h, 2, nk, n), jnp.float32),
        ],
        scratch_shapes=[pltpu.VMEM((tm, d), jnp.bfloat16)],
        compiler_params=_params("parallel", "arbitrary"),
        name="peer_q",
    )(x1, g, w_pq, keys)


def _cmp_exchange(a, i, l):
    hi, lo = jnp.maximum(a[i], a[l]), jnp.minimum(a[i], a[l])
    a[i], a[l] = hi, lo


def _bitonic_merge_desc(a):
    n = len(a)
    j = n // 2
    while j >= 1:
        for i in range(n):
            if i ^ j > i:
                _cmp_exchange(a, i, i ^ j)
        j //= 2


def _top_of_sorted_lists(a):
    k = len(a)
    dist = SUBLANES // 2
    while dist >= 1:
        b = [pltpu.roll(x, SUBLANES - dist, 0) for x in a]
        a = [jnp.maximum(a[i], b[k - 1 - i]) for i in range(k)]
        _bitonic_merge_desc(a)
        dist //= 2
    return a


def _sorted_topk(s, k, out_ref):
    assert s.shape[0] == k * SUBLANES and k & (k - 1) == 0
    a = [s[r * SUBLANES:(r + 1) * SUBLANES, :] for r in range(k)]
    size = 2
    while size <= k:
        j = size // 2
        while j >= 1:
            for i in range(k):
                l = i ^ j
                if l > i:
                    if (i & size) == 0:
                        _cmp_exchange(a, i, l)
                    else:
                        _cmp_exchange(a, l, i)
            j //= 2
        size *= 2
    a = _top_of_sorted_lists(a)
    for r in range(k):
        out_ref[r:r + 1, :] = a[r][0:1, :]
    return a[k - 1][0:1, :]


def _peer_topk_kernel(st_ref, s1m_ref, e1_ref, s2m_ref, e2n_ref, tau_ref, v1_ref, v2_ref):
    k = PEER_TOPK

    def one_head(hh, carry):
        s1 = st_ref[hh, 0]
        s2 = st_ref[hh, 1]
        th1 = _sorted_topk(s1, k, v1_ref)
        th2 = _sorted_topk(s2, k, v2_ref)
        s1m_ref[hh] = jnp.where(s1 >= th1, s1, _NEG_INF)
        s2m_ref[hh] = jnp.where(s2 >= th2, s2, _NEG_INF)
        v1 = v1_ref[...]
        v2 = v2_ref[...]
        m1 = v1[0:1, :]
        m2 = v2[0:1, :]
        e1_ref[hh] = jnp.exp(s1 - m1)
        e2 = jnp.exp(s2 - m2)
        ev1 = jnp.exp(v1 - m1)
        ev2 = jnp.exp(v2 - m2)

        cand = [v1 + v2[b:b + 1, :] for b in range(k)]
        lo = [c[0:SUBLANES, :] for c in cand]
        hi = [c[SUBLANES:2 * SUBLANES, :] for c in cand]
        top = [jnp.maximum(lo[i], hi[k - 1 - i]) for i in range(k)]
        _bitonic_merge_desc(top)
        tau = _top_of_sorted_lists(top)[k - 1][0:1, :]
        zsum = jnp.zeros_like(tau)
        for b in range(k):
            picked = jnp.where(cand[b] >= tau, ev1 * ev2[b:b + 1, :], 0.0)
            zsum = zsum + jnp.sum(picked, axis=0, keepdims=True)
        e2n_ref[hh] = e2 * (1.0 / zsum)
        tau_ref[hh] = tau
        return carry

    lax.fori_loop(0, st_ref.shape[0], one_head, 0)


def _peer_topk(st):
    h, _, nk, n = st.shape
    tl = _tile(n, 256)
    big = pl.BlockSpec((h, nk, tl), lambda i: (0, 0, i))
    shp = jax.ShapeDtypeStruct((h, nk, n), jnp.float32)
    return pl.pallas_call(
        _peer_topk_kernel,
        grid=(n // tl,),
        in_specs=[pl.BlockSpec((h, 2, nk, tl), lambda i: (0, 0, 0, i))],
        out_specs=[big, big, big, big, pl.BlockSpec((h, 1, tl), lambda i: (0, 0, i))],
        out_shape=[shp, shp, shp, shp, jax.ShapeDtypeStruct((h, 1, n), jnp.float32)],
        scratch_shapes=[pltpu.VMEM((PEER_TOPK, tl), jnp.float32),
                        pltpu.VMEM((PEER_TOPK, tl), jnp.float32)],
        compiler_params=_params("parallel"),
        name="peer_topk",
    )(st)


def _peer_ffn_kernel(x1_ref, xn_ref, u_ref, v_ref, s1m_ref, e1_ref, s2m_ref, e2n_ref, tau_ref, o_ref,
                     *, rows_per_step):
    e = pl.program_id(1)
    nh = s2m_ref.shape[0]
    nk = s2m_ref.shape[1]
    rblk = s1m_ref.shape[1]

    @pl.when(e == 0)
    def _():
        o_ref[...] = x1_ref[...]

    tm, d = xn_ref.shape
    base = (e % (rblk // rows_per_step)) * rows_per_step
    slabs = []
    zeros = []
    half_word = jnp.uint32(16)
    for r in range(rows_per_step):
        i8 = base + r
        acc = jnp.zeros((nk, tm), jnp.float32)
        for h in range(nh):
            ssum = s1m_ref[h, pl.ds(i8, 1), :] + s2m_ref[h]
            w = e1_ref[h, pl.ds(i8, 1), :] * e2n_ref[h]
            acc = acc + jnp.where(ssum >= tau_ref[h], w, 0.0)
        slabs.append(acc.T)
        bits = lax.bitcast_convert_type(acc, jnp.uint32)
        zf = lax.bitcast_convert_type(
            lax.shift_right_logical(lax.shift_right_logical(bits, half_word), half_word), jnp.float32)
        for c in range(tm // LANES):
            for g in range(nk // SUBLANES):
                blk = zf[g * SUBLANES:(g + 1) * SUBLANES, c * LANES:(c + 1) * LANES]
                zeros.append(jnp.concatenate([blk, blk], axis=0).astype(jnp.bfloat16))
    wsel = jnp.concatenate(slabs, axis=1) if len(slabs) > 1 else slabs[0]

    rb = 2 * SUBLANES
    kw = 2 * LANES
    n_pairs = (tm // rb) * (d // kw)
    cols = []
    for kb in range(d // kw):
        rows = []
        for mb in range(tm // rb):
            zb = zeros[((kb * (tm // rb) + mb) * len(zeros)) // n_pairs]
            rows.append(xn_ref[mb * rb:(mb + 1) * rb, kb * kw:(kb + 1) * kw] + jnp.concatenate([zb, zb], axis=1))
        cols.append(jnp.concatenate(rows, axis=0))
    hmat = _dot_nt(jnp.concatenate(cols, axis=1), u_ref[...])
    act = 0.5 * hmat * (1.0 + lax.erf(hmat * (1.0 / math.sqrt(2.0))))
    z = jnp.where(wsel != 0.0, wsel * act, 0.0).astype(jnp.bfloat16)
    o_ref[...] += _dot(z, v_ref[...])


def _peer_ffn(x1, xn, u_b, v_b, s1m, e1, s2m, e2n, tau):
    n, d = x1.shape
    ne = u_b.shape[0]
    nh, nk, _ = s1m.shape
    tm = _tile(n, 512)
    te = _tile(ne, 512)
    rows_per_step = te // nk
    rblk = max(SUBLANES, rows_per_step)
    steps_per_rblk = rblk // rows_per_step
    once = dict(pipeline_mode=pl.Buffered(1))
    return pl.pallas_call(
        functools.partial(_peer_ffn_kernel, rows_per_step=rows_per_step),
        grid=(n // tm, ne // te),
        in_specs=[
            pl.BlockSpec((tm, d), lambda i, j: (i, 0), **once),
            pl.BlockSpec((tm, d), lambda i, j: (i, 0)),
            pl.BlockSpec((te, d), lambda i, j: (j, 0)),
            pl.BlockSpec((te, d), lambda i, j: (j, 0)),
            pl.BlockSpec((nh, rblk, tm), lambda i, j: (0, j // steps_per_rblk, i)),
            pl.BlockSpec((nh, rblk, tm), lambda i, j: (0, j // steps_per_rblk, i)),
            pl.BlockSpec((nh, nk, tm), lambda i, j: (0, 0, i)),
            pl.BlockSpec((nh, nk, tm), lambda i, j: (0, 0, i)),
            pl.BlockSpec((nh, 1, tm), lambda i, j: (0, 0, i)),
        ],
        out_specs=pl.BlockSpec((tm, d), lambda i, j: (i, 0)),
        out_shape=jax.ShapeDtypeStruct((n, d), jnp.float32),
        compiler_params=_params("parallel", "arbitrary"),
        name="peer_ffn",
    )(x1, xn, u_b, v_b, s1m, e1, s2m, e2n, tau)


def _rope_tables(t, hd):
    half = ROPE_DIMS // 2
    inv = ROPE_THETA ** (-jnp.arange(half, dtype=jnp.float32) / half)
    ang = jnp.arange(t).astype(jnp.float32)[:, None] * inv[None, :]
    cos, sin = jnp.cos(ang), jnp.sin(ang)
    zeros = jnp.zeros((t, hd - 2 * half), jnp.float32)
    z16 = jnp.zeros((t, half), jnp.float32)
    cos_t = jnp.concatenate([cos, cos, jnp.ones_like(zeros)], axis=1)
    sa_t = jnp.concatenate([z16, sin, zeros], axis=1)
    sb_t = jnp.concatenate([-sin, z16, zeros], axis=1)
    return cos_t, sa_t, sb_t


def kernel(x, norm_mix_g, w_in, w_gate_up, b_gate, gla_norm_g, q_norm_g, k_norm_g, w_out, norm_ffn_g,
           peer_wq, peer_keys, peer_u, peer_v):
    batch, t, d = x.shape
    n = batch * t
    bf16 = jnp.bfloat16
    gla_qk_w = GLA_HEADS * GLA_DK
    gla_v_w = GLA_HEADS * GLA_DV
    lr0 = 2 * gla_qk_w + 2 * gla_v_w
    cos_t, sa_t, sb_t = _rope_tables(t, MOBA_HD)

    xcur = x.reshape(n, d)
    for l in range(w_in.shape[0]):
        wt = jnp.swapaxes(w_in[l], 0, 1)
        wlr_t = jnp.pad(wt[lr0:lr0 + GLA_GATE_RANK], ((0, LANES - GLA_GATE_RANK), (0, 0))).astype(bf16)
        wup_pad = jnp.pad(w_gate_up[l], ((0, LANES - GLA_GATE_RANK), (0, 0)))

        h, glr = _norm_lowrank(xcur, norm_mix_g[l][None, :], wlr_t)
        proj = _in_proj(h, wt, lr0, GLA_GATE_RANK)
        o_gla = _gla(proj, glr, wup_pad, b_gate[l][None, :], gla_norm_g[l][None, :], batch, t)
        o_moba, u_b, v_b = _moba(proj, q_norm_g[l][None, :], k_norm_g[l][None, :], cos_t, sa_t, sb_t, batch, t, lr0,
                                 peer_u[l], peer_v[l])
        x1 = _out_proj(xcur, o_gla, o_moba, w_out[l])

        xn, st = _peer_q(x1, norm_ffn_g[l][None, :], peer_wq[l].astype(bf16), peer_keys[l])
        s1m, e1, s2m, e2n, tau = _peer_topk(st)
        xcur = _peer_ffn(x1, xn, u_b, v_b, s1m, e1, s2m, e2n, tau)
    return xcur.reshape(batch, t, d)
```

```python
import functools
import math

import jax
import jax.numpy as jnp
from jax import lax
from jax.experimental import pallas as pl
from jax.experimental.pallas import tpu as pltpu

GLA_HEADS = 8
GLA_DK = 128
GLA_DV = 256
GLA_GATE_RANK = 16
GLA_GATE_NORMALIZER = 16.0
GLA_CHUNK = 64
MOBA_HEADS = 16
MOBA_HD = 128
MOBA_BLOCK = 256
MOBA_TOPK = 3
ROPE_THETA = 500000.0
ROPE_DIMS = MOBA_HD // 4
PEER_HEADS = 8
PEER_NKEYS = 128
PEER_DQ = 256
PEER_TOPK = 16
EPS = 1e-6

LANES = 128
SUBLANES = 8
VMEM_LIMIT_BYTES = 60 * 1024 * 1024
_HI = lax.Precision.HIGHEST
_NEG_INF = float("-inf")


def _dot(a, b, precision=None):
    return lax.dot_general(a, b, (((1,), (0,)), ((), ())), precision=precision,
                           preferred_element_type=jnp.float32)


def _dot_nt(a, b, precision=None):
    return lax.dot_general(a, b, (((1,), (1,)), ((), ())), precision=precision,
                           preferred_element_type=jnp.float32)


def _dot_tn(a, b, precision=None):
    return lax.dot_general(a, b, (((0,), (0,)), ((), ())), precision=precision,
                           preferred_element_type=jnp.float32)


def _tile(n, pref):
    t = min(n, pref)
    while n % t:
        t //= 2
    return t


def _params(*sem, flags=None):
    return pltpu.CompilerParams(dimension_semantics=sem, vmem_limit_bytes=VMEM_LIMIT_BYTES, flags=flags)


def _norm_kernel(x_ref, g_ref, wlrt_ref, h_ref, olr_ref):
    x = x_ref[...]
    r = lax.rsqrt(jnp.mean(x * x, axis=-1, keepdims=True) + EPS)
    hb = ((x * r) * g_ref[...]).astype(jnp.bfloat16)
    h_ref[...] = hb
    olr_ref[...] = _dot_nt(hb, wlrt_ref[...])


def _norm_lowrank(x2, g, wlr_t):
    n, d = x2.shape
    tm = _tile(n, 512)
    return pl.pallas_call(
        _norm_kernel,
        grid=(n // tm,),
        in_specs=[
            pl.BlockSpec((tm, d), lambda i: (i, 0)),
            pl.BlockSpec((1, d), lambda i: (0, 0)),
            pl.BlockSpec((LANES, d), lambda i: (0, 0)),
        ],
        out_specs=[
            pl.BlockSpec((tm, d), lambda i: (i, 0)),
            pl.BlockSpec((tm, LANES), lambda i: (i, 0)),
        ],
        out_shape=[
            jax.ShapeDtypeStruct((n, d), jnp.bfloat16),
            jax.ShapeDtypeStruct((n, LANES), jnp.float32),
        ],
        compiler_params=_params("parallel"),
        name="norm_mix",
    )(x2, g, wlr_t)


def _in_proj_kernel(h_ref, wt_ref, o_ref, wb_ref):
    @pl.when(pl.program_id(1) == 0)
    def _():
        wb_ref[...] = wt_ref[...].astype(jnp.bfloat16)

    o_ref[...] = _dot_nt(h_ref[...], wb_ref[...])


def _in_proj(h, wt, split, skip):
    n, d = h.shape
    cols = wt.shape[0] - skip
    tm = _tile(n, 512)
    tn = _tile(math.gcd(split, cols - split), 1024)
    return pl.pallas_call(
        _in_proj_kernel,
        grid=(cols // tn, n // tm),
        in_specs=[
            pl.BlockSpec((tm, d), lambda j, i: (i, 0)),
            pl.BlockSpec((pl.Element(tn), pl.Element(d)),
                         lambda j, i: (pl.multiple_of(j * tn + jnp.where(j * tn >= split, skip, 0), skip), 0)),
        ],
        out_specs=pl.BlockSpec((tm, tn), lambda j, i: (i, j)),
        out_shape=jax.ShapeDtypeStruct((n, cols), jnp.float32),
        scratch_shapes=[pltpu.VMEM((tn, d), jnp.bfloat16)],
        compiler_params=_params("parallel", "arbitrary"),
        name="in_proj",
    )(h, wt)


GLA_GROUP = 256


def _gla_kernel(q_ref, k_ref, v_ref, go_ref, glr_ref, wup_ref, bg_ref, ng_ref, o_ref,
                qd_ref, ki_ref, ks_ref, vb_ref, oacc_ref, dec_ref, upd_ref, sprev_ref):
    t, dk = q_ref.shape
    dv = v_ref.shape[1]
    c = GLA_CHUNK
    n = t // c
    grp = min(GLA_GROUP, t)
    scale = dk ** -0.5
    bf16 = jnp.bfloat16

    z = _dot(glr_ref[...], wup_ref[...], _HI) + bg_ref[...]
    log_a = (jnp.minimum(z, 0.0) - jnp.log1p(jnp.exp(-jnp.abs(z)))) / GLA_GATE_NORMALIZER
    row = lax.broadcasted_iota(jnp.int32, (grp, grp), 0)
    col = lax.broadcasted_iota(jnp.int32, (grp, grp), 1)
    same_chunk_causal = (row >= col) & ((row // c) == (col // c))
    tril = same_chunk_causal.astype(jnp.float32)
    bc = jnp.concatenate([_dot(tril, log_a[g * grp:(g + 1) * grp, :], _HI) for g in range(t // grp)], axis=0)
    bc3 = bc.reshape(n, c, dk)
    bl3 = bc3[:, c - 1:c, :]
    k3 = k_ref[...].reshape(n, c, dk)
    qd_ref[...] = (q_ref[...] * scale * jnp.exp(bc)).astype(bf16)
    ki_ref[...] = (k_ref[...] * jnp.exp(-bc)).astype(bf16)
    ks_ref[...] = (k3 * jnp.exp(bl3 - bc3)).reshape(t, dk).astype(bf16)
    dec_ref[...] = jnp.exp(bl3)
    vb_ref[...] = v_ref[...].astype(bf16)

    for g in range(t // grp):
        rows = pl.ds(g * grp, grp)
        att = jnp.where(same_chunk_causal, _dot_nt(qd_ref[rows, :], ki_ref[rows, :]), 0.0)
        oacc_ref[rows, :] = _dot(att.astype(bf16), vb_ref[rows, :])

    def upd_body(ci, carry):
        rows = pl.ds(pl.multiple_of(ci * c, c), c)
        upd_ref[ci] = _dot_tn(vb_ref[rows, :], ks_ref[rows, :])
        return carry

    lax.fori_loop(0, n, upd_body, 0, unroll=4)

    def scan_body(ci, st):
        sprev_ref[ci] = st.astype(bf16)
        return dec_ref[ci] * st + upd_ref[ci]

    lax.fori_loop(0, n, scan_body, jnp.zeros((dv, dk), jnp.float32), unroll=2)

    def inter_body(ci, carry):
        rows = pl.ds(pl.multiple_of(ci * c, c), c)
        oacc_ref[rows, :] += _dot_nt(qd_ref[rows, :], sprev_ref[ci])
        return carry

    lax.fori_loop(0, n, inter_body, 0, unroll=4)

    o = oacc_ref[...]
    r = lax.rsqrt(jnp.mean(o * o, axis=-1, keepdims=True) + EPS)
    go = go_ref[...]
    o_ref[...] = (((o * r) * ng_ref[...]) * (go * jax.nn.sigmoid(go))).astype(o_ref.dtype)


def _gla(proj, glr, wup_pad, b_gate, norm_g, batch, t):
    n = proj.shape[0]
    h, dk, dv = GLA_HEADS, GLA_DK, GLA_DV
    nc = t // GLA_CHUNK
    qk_w = h * dk
    kb = qk_w // dk
    vb = (2 * qk_w) // dv
    gb = (2 * qk_w + h * dv) // dv
    return pl.pallas_call(
        _gla_kernel,
        grid=(batch, h),
        in_specs=[
            pl.BlockSpec((t, dk), lambda b, i: (b, i)),
            pl.BlockSpec((t, dk), lambda b, i: (b, kb + i)),
            pl.BlockSpec((t, dv), lambda b, i: (b, vb + i)),
            pl.BlockSpec((t, dv), lambda b, i: (b, gb + i)),
            pl.BlockSpec((t, LANES), lambda b, i: (b, 0)),
            pl.BlockSpec((LANES, dk), lambda b, i: (0, i)),
            pl.BlockSpec((1, dk), lambda b, i: (0, i)),
            pl.BlockSpec((1, dv), lambda b, i: (0, 0)),
        ],
        out_specs=pl.BlockSpec((t, dv), lambda b, i: (b, i)),
        out_shape=jax.ShapeDtypeStruct((n, h * dv), jnp.bfloat16),
        scratch_shapes=[
            pltpu.VMEM((t, dk), jnp.bfloat16),
            pltpu.VMEM((t, dk), jnp.bfloat16),
            pltpu.VMEM((t, dk), jnp.bfloat16),
            pltpu.VMEM((t, dv), jnp.bfloat16),
            pltpu.VMEM((t, dv), jnp.float32),
            pltpu.VMEM((nc, 1, dk), jnp.float32),
            pltpu.VMEM((nc, dv, dk), jnp.float32),
            pltpu.VMEM((nc, dv, dk), jnp.bfloat16),
        ],
        compiler_params=_params("parallel", "parallel"),
        name="gla",
    )(proj, proj, proj, proj, glr, wup_pad, b_gate, norm_g)


MASK_VALUE = -1e30


def _moba_kernel(q_ref, k_ref, v_ref, qg_ref, kg_ref, cos_ref, sa_ref, sb_ref, ta_ref, tb_ref,
                 o_ref, tab_ref, tbb_ref, qa_ref, ka_ref, vb_ref, bias_ref):
    t, hd = q_ref.shape
    blk = MOBA_BLOCK
    nb = t // blk
    half = ROPE_DIMS // 2
    bf16 = jnp.bfloat16

    tab_ref[...] = ta_ref[...].astype(bf16)
    tbb_ref[...] = tb_ref[...].astype(bf16)

    def prep(x, g, scale):
        r = lax.rsqrt(jnp.mean(x * x, axis=-1, keepdims=True) + EPS)
        xn = (x * r) * g
        y = (xn * cos_ref[...] + pltpu.roll(xn, half, 1) * sa_ref[...]
             + pltpu.roll(xn, hd - half, 1) * sb_ref[...])
        return y * scale

    qh = prep(q_ref[...], qg_ref[...], hd ** -0.5)
    kh = prep(k_ref[...], kg_ref[...], 1.0)
    vb_ref[...] = v_ref[...].astype(bf16)

    km = jnp.sum(kh.reshape(nb, blk, hd), axis=1) * (1.0 / blk)
    gate_t = _dot_nt(km, qh, _HI)
    kblk = lax.broadcasted_iota(jnp.int32, (nb, t), 0)
    qblk = lax.broadcasted_iota(jnp.int32, (nb, t), 1) // blk
    past = kblk < qblk
    gm = jnp.where(past, gate_t, _NEG_INF)
    rank = jnp.zeros((nb, t), jnp.float32)
    for n2 in range(nb):
        other = gm[n2:n2 + 1, :]
        beats = (other > gm) | ((other == gm) & (kblk > n2))
        rank = rank + beats.astype(jnp.float32)
    allowed = (past & (rank < float(MOBA_TOPK))) | (kblk == qblk)
    bias_ref[...] = jnp.zeros_like(bias_ref)
    bias_ref[0:nb, :] = jnp.where(allowed, 0.0, MASK_VALUE)

    qa_ref[:, 0:hd] = qh.astype(bf16)
    qa_ref[:, hd:2 * hd] = bias_ref[...].T.astype(bf16)
    ka_ref[:, 0:hd] = kh.astype(bf16)
    lane = lax.broadcasted_iota(jnp.int32, (t, hd), 1)
    rblk = lax.broadcasted_iota(jnp.int32, (t, hd), 0) // blk
    ka_ref[:, hd:2 * hd] = (lane == rblk).astype(bf16)

    lrow = lax.broadcasted_iota(jnp.int32, (blk, blk), 0)
    lcol = lax.broadcasted_iota(jnp.int32, (blk, blk), 1)
    causal = lrow >= lcol
    for j in range(nb):
        rows = pl.ds(j * blk, blk)
        nkeys = (j + 1) * blk
        s = _dot_nt(qa_ref[rows, :], ka_ref[0:nkeys, :])
        own = jnp.where(causal, s[:, j * blk:nkeys], MASK_VALUE)
        s = own if j == 0 else jnp.concatenate([s[:, :j * blk], own], axis=1)
        m = jnp.max(s, axis=-1, keepdims=True)
        p = jnp.exp(s - m)
        l = jnp.sum(p, axis=-1, keepdims=True)
        o = _dot(p.astype(bf16), vb_ref[0:nkeys, :])
        o_ref[rows, :] = (o * (1.0 / l)).astype(o_ref.dtype)


def _moba(proj, q_g, k_g, cos_t, sa_t, sb_t, batch, t, col0, table_a, table_b):
    n = proj.shape[0]
    h, hd = MOBA_HEADS, MOBA_HD
    qb = col0 // hd
    kb = qb + h
    vb = kb + h
    tab = pl.BlockSpec((t, hd), lambda b, i: (0, 0))
    trows, tcols = table_a.shape
    assert table_b.shape == table_a.shape and trows % (batch * h) == 0
    slab = pl.BlockSpec((trows // (batch * h), tcols), lambda b, i: (b * h + i, 0))
    return pl.pallas_call(
        _moba_kernel,
        grid=(batch, h),
        in_specs=[
            pl.BlockSpec((t, hd), lambda b, i: (b, qb + i)),
            pl.BlockSpec((t, hd), lambda b, i: (b, kb + i)),
            pl.BlockSpec((t, hd), lambda b, i: (b, vb + i)),
            pl.BlockSpec((1, hd), lambda b, i: (0, 0)),
            pl.BlockSpec((1, hd), lambda b, i: (0, 0)),
            tab, tab, tab, slab, slab,
        ],
        out_specs=[pl.BlockSpec((t, hd), lambda b, i: (b, i)), slab, slab],
        out_shape=[jax.ShapeDtypeStruct((n, h * hd), jnp.bfloat16),
                   jax.ShapeDtypeStruct(table_a.shape, jnp.bfloat16),
                   jax.ShapeDtypeStruct(table_b.shape, jnp.bfloat16)],
        scratch_shapes=[
            pltpu.VMEM((t, 2 * hd), jnp.bfloat16),
            pltpu.VMEM((t, 2 * hd), jnp.bfloat16),
            pltpu.VMEM((t, hd), jnp.bfloat16),
            pltpu.VMEM((LANES, t), jnp.float32),
        ],
        compiler_params=_params("parallel", "parallel"),
        name="moba",
    )(proj, proj, proj, q_g, k_g, cos_t, sa_t, sb_t, table_a, table_b)


def _out_proj_kernel(x_ref, a_ref, b_ref, wa_ref, wb_ref, o_ref, wab_ref, wbb_ref):
    @pl.when(pl.program_id(1) == 0)
    def _():
        wab_ref[...] = wa_ref[...].astype(jnp.bfloat16)
        wbb_ref[...] = wb_ref[...].astype(jnp.bfloat16)

    o_ref[...] = x_ref[...] + _dot(a_ref[...], wab_ref[...]) + _dot(b_ref[...], wbb_ref[...])


def _out_proj(x2, o_gla, o_moba, w):
    n, d = x2.shape
    ka, kb = o_gla.shape[1], o_moba.shape[1]
    assert ka == kb and w.shape[0] == ka + kb
    tm = _tile(n, 512)
    tn = _tile(d, 1024)
    return pl.pallas_call(
        _out_proj_kernel,
        grid=(d // tn, n // tm),
        in_specs=[
            pl.BlockSpec((tm, tn), lambda j, i: (i, j)),
            pl.BlockSpec((tm, ka), lambda j, i: (i, 0)),
            pl.BlockSpec((tm, kb), lambda j, i: (i, 0)),
            pl.BlockSpec((ka, tn), lambda j, i: (0, j)),
            pl.BlockSpec((kb, tn), lambda j, i: (1, j)),
        ],
        out_specs=pl.BlockSpec((tm, tn), lambda j, i: (i, j)),
        out_shape=jax.ShapeDtypeStruct((n, d), jnp.float32),
        scratch_shapes=[pltpu.VMEM((ka, tn), jnp.bfloat16), pltpu.VMEM((kb, tn), jnp.bfloat16)],
        compiler_params=_params("parallel", "arbitrary"),
        name="out_proj",
    )(x2, o_gla, o_moba, w, w)


def _peer_q_kernel(x_ref, g_ref, w_ref, keys_ref, xn_ref, st_ref, hb_ref):
    @pl.when(pl.program_id(1) == 0)
    def _():
        x = x_ref[...]
        r = lax.rsqrt(jnp.mean(x * x, axis=-1, keepdims=True) + EPS)
        hb = ((x * r) * g_ref[...]).astype(jnp.bfloat16)
        hb_ref[...] = hb
        xn_ref[...] = hb

    q = _dot(hb_ref[...], w_ref[...])
    half = keys_ref.shape[3]
    for hh in range(keys_ref.shape[0]):
        for p in range(2):
            c0 = (2 * hh + p) * half
            st_ref[hh, p] = _dot_nt(keys_ref[hh, p].astype(jnp.bfloat16), q[:, c0:c0 + half].astype(jnp.bfloat16))


PEER_Q_HEADS_PER_STEP = 2


def _peer_q(x1, g, w_pq, keys):
    n, d = x1.shape
    h, dq, nk = PEER_HEADS, PEER_DQ, PEER_NKEYS
    hp = math.gcd(h, PEER_Q_HEADS_PER_STEP)
    tm = _tile(n, 512)
    return pl.pallas_call(
        _peer_q_kernel,
        grid=(n // tm, h // hp),
        in_specs=[
            pl.BlockSpec((tm, d), lambda i, j: (i, 0)),
            pl.BlockSpec((1, d), lambda i, j: (0, 0)),
            pl.BlockSpec((d, hp * dq), lambda i, j: (0, j)),
            pl.BlockSpec((hp, 2, nk, dq // 2), lambda i, j: (j, 0, 0, 0)),
        ],
        out_specs=[
            pl.BlockSpec((tm, d), lambda i, j: (i, 0)),
            pl.BlockSpec((hp, 2, nk, tm), lambda i, j: (j, 0, 0, i)),
        ],
        out_shape=[
            jax.ShapeDtypeStruct((n, d), jnp.bfloat16),
            jax.ShapeDtypeStruct((h, 2, nk, n), jnp.float32),
        ],
        scratch_shapes=[pltpu.VMEM((tm, d), jnp.bfloat16)],
        compiler_params=_params("parallel", "arbitrary"),
        name="peer_q",
    )(x1, g, w_pq, keys)


def _cmp_exchange(a, i, l):
    hi, lo = jnp.maximum(a[i], a[l]), jnp.minimum(a[i], a[l])
    a[i], a[l] = hi, lo


def _bitonic_merge_desc(a):
    n = len(a)
    j = n // 2
    while j >= 1:
        for i in range(n):
            if i ^ j > i:
                _cmp_exchange(a, i, i ^ j)
        j //= 2


def _top_of_sorted_lists(a):
    k = len(a)
    dist = SUBLANES // 2
    while dist >= 1:
        b = [pltpu.roll(x, SUBLANES - dist, 0) for x in a]
        a = [jnp.maximum(a[i], b[k - 1 - i]) for i in range(k)]
        _bitonic_merge_desc(a)
        dist //= 2
    return a


def _sorted_topk(s, k, out_ref):
    assert s.shape[0] == k * SUBLANES and k & (k - 1) == 0
    a = [s[r * SUBLANES:(r + 1) * SUBLANES, :] for r in range(k)]
    size = 2
    while size <= k:
        j = size // 2
        while j >= 1:
            for i in range(k):
                l = i ^ j
                if l > i:
                    if (i & size) == 0:
                        _cmp_exchange(a, i, l)
                    else:
                        _cmp_exchange(a, l, i)
            j //= 2
        size *= 2
    a = _top_of_sorted_lists(a)
    for r in range(k):
        out_ref[r:r + 1, :] = a[r][0:1, :]
    return a[k - 1][0:1, :]


def _peer_topk_kernel(st_ref, s1m_ref, e1_ref, s2m_ref, e2n_ref, tau_ref, v1_ref, v2_ref):
    k = PEER_TOPK

    def one_head(hh, carry):
        s1 = st_ref[hh, 0]
        s2 = st_ref[hh, 1]
        th1 = _sorted_topk(s1, k, v1_ref)
        th2 = _sorted_topk(s2, k, v2_ref)
        s1m_ref[hh] = jnp.where(s1 >= th1, s1, _NEG_INF)
        s2m_ref[hh] = jnp.where(s2 >= th2, s2, _NEG_INF)
        v1 = v1_ref[...]
        v2 = v2_ref[...]
        m1 = v1[0:1, :]
        m2 = v2[0:1, :]
        e1_ref[hh] = jnp.exp(s1 - m1)
        e2 = jnp.exp(s2 - m2)
        ev1 = jnp.exp(v1 - m1)
        ev2 = jnp.exp(v2 - m2)

        cand = [v1 + v2[b:b + 1, :] for b in range(k)]
        lo = [c[0:SUBLANES, :] for c in cand]
        hi = [c[SUBLANES:2 * SUBLANES, :] for c in cand]
        top = [jnp.maximum(lo[i], hi[k - 1 - i]) for i in range(k)]
        _bitonic_merge_desc(top)
        tau = _top_of_sorted_lists(top)[k - 1][0:1, :]
        zsum = jnp.zeros_like(tau)
        for b in range(k):
            picked = jnp.where(cand[b] >= tau, ev1 * ev2[b:b + 1, :], 0.0)
            zsum = zsum + jnp.sum(picked, axis=0, keepdims=True)
        e2n_ref[hh] = e2 * (1.0 / zsum)
        tau_ref[hh] = tau
        return carry

    lax.fori_loop(0, st_ref.shape[0], one_head, 0)


def _peer_topk(st):
    h, _, nk, n = st.shape
    tl = _tile(n, 256)
    big = pl.BlockSpec((h, nk, tl), lambda i: (0, 0, i))
    shp = jax.ShapeDtypeStruct((h, nk, n), jnp.float32)
    return pl.pallas_call(
        _peer_topk_kernel,
        grid=(n // tl,),
        in_specs=[pl.BlockSpec((h, 2, nk, tl), lambda i: (0, 0, 0, i))],
        out_specs=[big, big, big, big, pl.BlockSpec((h, 1, tl), lambda i: (0, 0, i))],
        out_shape=[shp, shp, shp, shp, jax.ShapeDtypeStruct((h, 1, n), jnp.float32)],
        scratch_shapes=[pltpu.VMEM((PEER_TOPK, tl), jnp.float32),
                        pltpu.VMEM((PEER_TOPK, tl), jnp.float32)],
        compiler_params=_params("parallel"),
        name="peer_topk",
    )(st)


def _peer_ffn_kernel(x1_ref, xn_ref, u_ref, v_ref, s1m_ref, e1_ref, s2m_ref, e2n_ref, tau_ref, o_ref,
                     *, rows_per_step):
    e = pl.program_id(1)
    nh = s2m_ref.shape[0]
    nk = s2m_ref.shape[1]
    rblk = s1m_ref.shape[1]

    @pl.when(e == 0)
    def _():
        o_ref[...] = x1_ref[...]

    tm, d = xn_ref.shape
    base = (e % (rblk // rows_per_step)) * rows_per_step
    slabs = []
    zeros = []
    half_word = jnp.uint32(16)
    for r in range(rows_per_step):
        i8 = base + r
        acc = jnp.zeros((nk, tm), jnp.float32)
        for h in range(nh):
            ssum = s1m_ref[h, pl.ds(i8, 1), :] + s2m_ref[h]
            w = e1_ref[h, pl.ds(i8, 1), :] * e2n_ref[h]
            acc = acc + jnp.where(ssum >= tau_ref[h], w, 0.0)
        slabs.append(acc.T)
        bits = lax.bitcast_convert_type(acc, jnp.uint32)
        zf = lax.bitcast_convert_type(
            lax.shift_right_logical(lax.shift_right_logical(bits, half_word), half_word), jnp.float32)
        for c in range(tm // LANES):
            for g in range(nk // SUBLANES):
                blk = zf[g * SUBLANES:(g + 1) * SUBLANES, c * LANES:(c + 1) * LANES]
                zeros.append(jnp.concatenate([blk, blk], axis=0).astype(jnp.bfloat16))
    wsel = jnp.concatenate(slabs, axis=1) if len(slabs) > 1 else slabs[0]

    rb = 2 * SUBLANES
    kw = 2 * LANES
    n_pairs = (tm // rb) * (d // kw)
    cols = []
    for kb in range(d // kw):
        rows = []
        for mb in range(tm // rb):
            zb = zeros[((kb * (tm // rb) + mb) * len(zeros)) // n_pairs]
            rows.append(xn_ref[mb * rb:(mb + 1) * rb, kb * kw:(kb + 1) * kw] + jnp.concatenate([zb, zb], axis=1))
        cols.append(jnp.concatenate(rows, axis=0))
    hmat = _dot_nt(jnp.concatenate(cols, axis=1), u_ref[...])
    act = 0.5 * hmat * (1.0 + lax.erf(hmat * (1.0 / math.sqrt(2.0))))
    z = jnp.where(wsel != 0.0, wsel * act, 0.0).astype(jnp.bfloat16)
    o_ref[...] += _dot(z, v_ref[...])


def _peer_ffn(x1, xn, u_b, v_b, s1m, e1, s2m, e2n, tau):
    n, d = x1.shape
    ne = u_b.shape[0]
    nh, nk, _ = s1m.shape
    tm = _tile(n, 512)
    te = _tile(ne, 512)
    rows_per_step = te // nk
    rblk = max(SUBLANES, rows_per_step)
    steps_per_rblk = rblk // rows_per_step
    once = dict(pipeline_mode=pl.Buffered(1))
    return pl.pallas_call(
        functools.partial(_peer_ffn_kernel, rows_per_step=rows_per_step),
        grid=(n // tm, ne // te),
        in_specs=[
            pl.BlockSpec((tm, d), lambda i, j: (i, 0), **once),
            pl.BlockSpec((tm, d), lambda i, j: (i, 0)),
            pl.BlockSpec((te, d), lambda i, j: (j, 0)),
            pl.BlockSpec((te, d), lambda i, j: (j, 0)),
            pl.BlockSpec((nh, rblk, tm), lambda i, j: (0, j // steps_per_rblk, i)),
            pl.BlockSpec((nh, rblk, tm), lambda i, j: (0, j // steps_per_rblk, i)),
            pl.BlockSpec((nh, nk, tm), lambda i, j: (0, 0, i)),
            pl.BlockSpec((nh, nk, tm), lambda i, j: (0, 0, i)),
            pl.BlockSpec((nh, 1, tm), lambda i, j: (0, 0, i)),
        ],
        out_specs=pl.BlockSpec((tm, d), lambda i, j: (i, 0)),
        out_shape=jax.ShapeDtypeStruct((n, d), jnp.float32),
        compiler_params=_params("parallel", "arbitrary"),
        name="peer_ffn",
    )(x1, xn, u_b, v_b, s1m, e1, s2m, e2n, tau)


def _rope_tables(t, hd):
    half = ROPE_DIMS // 2
    inv = ROPE_THETA ** (-jnp.arange(half, dtype=jnp.float32) / half)
    ang = jnp.arange(t).astype(jnp.float32)[:, None] * inv[None, :]
    cos, sin = jnp.cos(ang), jnp.sin(ang)
    zeros = jnp.zeros((t, hd - 2 * half), jnp.float32)
    z16 = jnp.zeros((t, half), jnp.float32)
    cos_t = jnp.concatenate([cos, cos, jnp.ones_like(zeros)], axis=1)
    sa_t = jnp.concatenate([z16, sin, zeros], axis=1)
    sb_t = jnp.concatenate([-sin, z16, zeros], axis=1)
    return cos_t, sa_t, sb_t


def kernel(x, norm_mix_g, w_in, w_gate_up, b_gate, gla_norm_g, q_norm_g, k_norm_g, w_out, norm_ffn_g,
           peer_wq, peer_keys, peer_u, peer_v):
    batch, t, d = x.shape
    n = batch * t
    bf16 = jnp.bfloat16
    gla_qk_w = GLA_HEADS * GLA_DK
    gla_v_w = GLA_HEADS * GLA_DV
    lr0 = 2 * gla_qk_w + 2 * gla_v_w
    cos_t, sa_t, sb_t = _rope_tables(t, MOBA_HD)

    xcur = x.reshape(n, d)
    for l in range(w_in.shape[0]):
        wt = jnp.swapaxes(w_in[l], 0, 1)
        wlr_t = jnp.pad(wt[lr0:lr0 + GLA_GATE_RANK], ((0, LANES - GLA_GATE_RANK), (0, 0))).astype(bf16)
        wup_pad = jnp.pad(w_gate_up[l], ((0, LANES - GLA_GATE_RANK), (0, 0)))

        h, glr = _norm_lowrank(xcur, norm_mix_g[l][None, :], wlr_t)
        proj = _in_proj(h, wt, lr0, GLA_GATE_RANK)
        o_gla = _gla(proj, glr, wup_pad, b_gate[l][None, :], gla_norm_g[l][None, :], batch, t)
        o_moba, u_b, v_b = _moba(proj, q_norm_g[l][None, :], k_norm_g[l][None, :], cos_t, sa_t, sb_t, batch, t, lr0,
                                 peer_u[l], peer_v[l])
        x1 = _out_proj(xcur, o_gla, o_moba, w_out[l])

        xn, st = _peer_q(x1, norm_ffn_g[l][None, :], peer_wq[l].astype(bf16), peer_keys[l])
        s1m, e1, s2m, e2n, tau = _peer_topk(st)
        xcur = _peer_ffn(x1, xn, u_b, v_b, s1m, e1, s2m, e2n, tau)
    return xcur.reshape(batch, t, d)
```

```python
import functools
import math

import jax
import jax.numpy as jnp
from jax import lax
from jax.experimental import pallas as pl
from jax.experimental.pallas import tpu as pltpu

GLA_HEADS = 8
GLA_DK = 128
GLA_DV = 256
GLA_GATE_RANK = 16
GLA_GATE_NORMALIZER = 16.0
GLA_CHUNK = 64
MOBA_HEADS = 16
MOBA_HD = 128
MOBA_BLOCK = 256
MOBA_TOPK = 3
ROPE_THETA = 500000.0
ROPE_DIMS = MOBA_HD // 4
PEER_HEADS = 8
PEER_NKEYS = 128
PEER_DQ = 256
PEER_TOPK = 16
EPS = 1e-6

LANES = 128
SUBLANES = 8
VMEM_LIMIT_BYTES = 60 * 1024 * 1024
_HI = lax.Precision.HIGHEST
_NEG_INF = float("-inf")


def _dot(a, b, precision=None):
    return lax.dot_general(a, b, (((1,), (0,)), ((), ())), precision=precision,
                           preferred_element_type=jnp.float32)


def _dot_nt(a, b, precision=None):
    return lax.dot_general(a, b, (((1,), (1,)), ((), ())), precision=precision,
                           preferred_element_type=jnp.float32)


def _dot_tn(a, b, precision=None):
    return lax.dot_general(a, b, (((0,), (0,)), ((), ())), precision=precision,
                           preferred_element_type=jnp.float32)


def _tile(n, pref):
    t = min(n, pref)
    while n % t:
        t //= 2
    return t


def _params(*sem, flags=None):
    return pltpu.CompilerParams(dimension_semantics=sem, vmem_limit_bytes=VMEM_LIMIT_BYTES, flags=flags)


def _norm_kernel(x_ref, g_ref, wlrt_ref, h_ref, olr_ref):
    x = x_ref[...]
    r = lax.rsqrt(jnp.mean(x * x, axis=-1, keepdims=True) + EPS)
    hb = ((x * r) * g_ref[...]).astype(jnp.bfloat16)
    h_ref[...] = hb
    olr_ref[...] = _dot_nt(hb, wlrt_ref[...])


def _norm_lowrank(x2, g, wlr_t):
    n, d = x2.shape
    tm = _tile(n, 512)
    return pl.pallas_call(
        _norm_kernel,
        grid=(n // tm,),
        in_specs=[
            pl.BlockSpec((tm, d), lambda i: (i, 0)),
            pl.BlockSpec((1, d), lambda i: (0, 0)),
            pl.BlockSpec((LANES, d), lambda i: (0, 0)),
        ],
        out_specs=[
            pl.BlockSpec((tm, d), lambda i: (i, 0)),
            pl.BlockSpec((tm, LANES), lambda i: (i, 0)),
        ],
        out_shape=[
            jax.ShapeDtypeStruct((n, d), jnp.bfloat16),
            jax.ShapeDtypeStruct((n, LANES), jnp.float32),
        ],
        compiler_params=_params("parallel"),
        name="norm_mix",
    )(x2, g, wlr_t)


def _in_proj_kernel(h_ref, wt_ref, o_ref, wb_ref):
    @pl.when(pl.program_id(1) == 0)
    def _():
        wb_ref[...] = wt_ref[...].astype(jnp.bfloat16)

    o_ref[...] = _dot_nt(h_ref[...], wb_ref[...])


def _in_proj(h, wt, split, skip):
    n, d = h.shape
    cols = wt.shape[0] - skip
    tm = _tile(n, 512)
    tn = _tile(math.gcd(split, cols - split), 1024)
    return pl.pallas_call(
        _in_proj_kernel,
        grid=(cols // tn, n // tm),
        in_specs=[
            pl.BlockSpec((tm, d), lambda j, i: (i, 0)),
            pl.BlockSpec((pl.Element(tn), pl.Element(d)),
                         lambda j, i: (pl.multiple_of(j * tn + jnp.where(j * tn >= split, skip, 0), skip), 0)),
        ],
        out_specs=pl.BlockSpec((tm, tn), lambda j, i: (i, j)),
        out_shape=jax.ShapeDtypeStruct((n, cols), jnp.float32),
        scratch_shapes=[pltpu.VMEM((tn, d), jnp.bfloat16)],
        compiler_params=_params("parallel", "arbitrary"),
        name="in_proj",
    )(h, wt)


GLA_GROUP = 256


def _gla_kernel(q_ref, k_ref, v_ref, go_ref, glr_ref, wup_ref, bg_ref, ng_ref, o_ref,
                qd_ref, ki_ref, ks_ref, vb_ref, oacc_ref, dec_ref, upd_ref, sprev_ref):
    t, dk = q_ref.shape
    dv = v_ref.shape[1]
    c = GLA_CHUNK
    n = t // c
    grp = min(GLA_GROUP, t)
    scale = dk ** -0.5
    bf16 = jnp.bfloat16

    z = _dot(glr_ref[...], wup_ref[...], _HI) + bg_ref[...]
    log_a = (jnp.minimum(z, 0.0) - jnp.log(1.0 + jnp.exp(-jnp.abs(z)))) / GLA_GATE_NORMALIZER
    row = lax.broadcasted_iota(jnp.int32, (grp, grp), 0)
    col = lax.broadcasted_iota(jnp.int32, (grp, grp), 1)
    same_chunk_causal = (row >= col) & ((row // c) == (col // c))
    tril = same_chunk_causal.astype(jnp.float32)
    bc = jnp.concatenate([_dot(tril, log_a[g * grp:(g + 1) * grp, :], _HI) for g in range(t // grp)], axis=0)
    bc3 = bc.reshape(n, c, dk)
    bl3 = bc3[:, c - 1:c, :]
    k3 = k_ref[...].reshape(n, c, dk)
    qd_ref[...] = (q_ref[...] * scale * jnp.exp(bc)).astype(bf16)
    ki_ref[...] = (k_ref[...] * jnp.exp(-bc)).astype(bf16)
    ks_ref[...] = (k3 * jnp.exp(bl3 - bc3)).reshape(t, dk).astype(bf16)
    dec_ref[...] = jnp.exp(bl3)
    vb_ref[...] = v_ref[...].astype(bf16)

    for g in range(t // grp):
        rows = pl.ds(g * grp, grp)
        att = jnp.where(same_chunk_causal, _dot_nt(qd_ref[rows, :], ki_ref[rows, :]), 0.0)
        oacc_ref[rows, :] = _dot(att.astype(bf16), vb_ref[rows, :])

    def upd_body(ci, carry):
        rows = pl.ds(pl.multiple_of(ci * c, c), c)
        upd_ref[ci] = _dot_tn(vb_ref[rows, :], ks_ref[rows, :])
        return carry

    lax.fori_loop(0, n, upd_body, 0, unroll=4)

    def scan_body(ci, st):
        sprev_ref[ci] = st.astype(bf16)
        return dec_ref[ci] * st + upd_ref[ci]

    lax.fori_loop(0, n, scan_body, jnp.zeros((dv, dk), jnp.float32), unroll=2)

    def inter_body(ci, carry):
        rows = pl.ds(pl.multiple_of(ci * c, c), c)
        oacc_ref[rows, :] += _dot_nt(qd_ref[rows, :], sprev_ref[ci])
        return carry

    lax.fori_loop(0, n, inter_body, 0, unroll=4)

    o = oacc_ref[...]
    r = lax.rsqrt(jnp.mean(o * o, axis=-1, keepdims=True) + EPS)
    go = go_ref[...]
    o_ref[...] = (((o * r) * ng_ref[...]) * (go * jax.nn.sigmoid(go))).astype(o_ref.dtype)


def _gla(proj, glr, wup_pad, b_gate, norm_g, batch, t):
    n = proj.shape[0]
    h, dk, dv = GLA_HEADS, GLA_DK, GLA_DV
    nc = t // GLA_CHUNK
    qk_w = h * dk
    kb = qk_w // dk
    vb = (2 * qk_w) // dv
    gb = (2 * qk_w + h * dv) // dv
    return pl.pallas_call(
        _gla_kernel,
        grid=(batch, h),
        in_specs=[
            pl.BlockSpec((t, dk), lambda b, i: (b, i)),
            pl.BlockSpec((t, dk), lambda b, i: (b, kb + i)),
            pl.BlockSpec((t, dv), lambda b, i: (b, vb + i)),
            pl.BlockSpec((t, dv), lambda b, i: (b, gb + i)),
            pl.BlockSpec((t, LANES), lambda b, i: (b, 0)),
            pl.BlockSpec((LANES, dk), lambda b, i: (0, i)),
            pl.BlockSpec((1, dk), lambda b, i: (0, i)),
            pl.BlockSpec((1, dv), lambda b, i: (0, 0)),
        ],
        out_specs=pl.BlockSpec((t, dv), lambda b, i: (b, i)),
        out_shape=jax.ShapeDtypeStruct((n, h * dv), jnp.bfloat16),
        scratch_shapes=[
            pltpu.VMEM((t, dk), jnp.bfloat16),
            pltpu.VMEM((t, dk), jnp.bfloat16),
            pltpu.VMEM((t, dk), jnp.bfloat16),
            pltpu.VMEM((t, dv), jnp.bfloat16),
            pltpu.VMEM((t, dv), jnp.float32),
            pltpu.VMEM((nc, 1, dk), jnp.float32),
            pltpu.VMEM((nc, dv, dk), jnp.float32),
            pltpu.VMEM((nc, dv, dk), jnp.bfloat16),
        ],
        compiler_params=_params("parallel", "parallel"),
        name="gla",
    )(proj, proj, proj, proj, glr, wup_pad, b_gate, norm_g)


MASK_VALUE = -1e30
LOG2_E = math.log2(math.e)


def _moba_kernel(q_ref, k_ref, v_ref, qg_ref, kg_ref, cos_ref, sa_ref, sb_ref, ta_ref, tb_ref,
                 o_ref, tab_ref, tbb_ref, qa_ref, ka_ref, vb_ref, bias_ref):
    t, hd = q_ref.shape
    blk = MOBA_BLOCK
    nb = t // blk
    half = ROPE_DIMS // 2
    bf16 = jnp.bfloat16

    tab_ref[...] = ta_ref[...].astype(bf16)
    tbb_ref[...] = tb_ref[...].astype(bf16)

    def prep(x, g, scale):
        r = lax.rsqrt(jnp.mean(x * x, axis=-1, keepdims=True) + EPS)
        xn = (x * r) * g
        y = (xn * cos_ref[...] + pltpu.roll(xn, half, 1) * sa_ref[...]
             + pltpu.roll(xn, hd - half, 1) * sb_ref[...])
        return y * scale

    qs = prep(q_ref[...], qg_ref[...], hd ** -0.5)
    qh = qs * LOG2_E
    kh = prep(k_ref[...], kg_ref[...], 1.0)
    vb_ref[...] = v_ref[...].astype(bf16)

    km = jnp.sum(kh.reshape(nb, blk, hd), axis=1) * (1.0 / blk)
    gate_t = _dot_nt(km, qs, _HI)
    kblk = lax.broadcasted_iota(jnp.int32, (nb, t), 0)
    qblk = lax.broadcasted_iota(jnp.int32, (nb, t), 1) // blk
    past = kblk < qblk
    gm = jnp.where(past, gate_t, _NEG_INF)
    rank = jnp.zeros((nb, t), jnp.float32)
    for n2 in range(nb):
        other = gm[n2:n2 + 1, :]
        beats = (other > gm) | ((other == gm) & (kblk > n2))
        rank = rank + beats.astype(jnp.float32)
    allowed = (past & (rank < float(MOBA_TOPK))) | (kblk == qblk)
    bias_ref[...] = jnp.zeros_like(bias_ref)
    bias_ref[0:nb, :] = jnp.where(allowed, 0.0, MASK_VALUE)

    qa_ref[:, 0:hd] = qh.astype(bf16)
    qa_ref[:, hd:2 * hd] = bias_ref[...].T.astype(bf16)
    ka_ref[:, 0:hd] = kh.astype(bf16)
    lane = lax.broadcasted_iota(jnp.int32, (t, hd), 1)
    rblk = lax.broadcasted_iota(jnp.int32, (t, hd), 0) // blk
    ka_ref[:, hd:2 * hd] = (lane == rblk).astype(bf16)

    lrow = lax.broadcasted_iota(jnp.int32, (blk, blk), 0)
    lcol = lax.broadcasted_iota(jnp.int32, (blk, blk), 1)
    causal = lrow >= lcol
    for j in range(nb):
        rows = pl.ds(j * blk, blk)
        nkeys = (j + 1) * blk
        s = _dot_nt(qa_ref[rows, :], ka_ref[0:nkeys, :])
        own = jnp.where(causal, s[:, j * blk:nkeys], MASK_VALUE)
        s = own if j == 0 else jnp.concatenate([s[:, :j * blk], own], axis=1)
        m = jnp.max(s, axis=-1, keepdims=True)
        p = jnp.exp2(s - m)
        l = jnp.sum(p, axis=-1, keepdims=True)
        o = _dot(p.astype(bf16), vb_ref[0:nkeys, :])
        o_ref[rows, :] = (o * (1.0 / l)).astype(o_ref.dtype)


def _moba(proj, q_g, k_g, cos_t, sa_t, sb_t, batch, t, col0, table_a, table_b):
    n = proj.shape[0]
    h, hd = MOBA_HEADS, MOBA_HD
    qb = col0 // hd
    kb = qb + h
    vb = kb + h
    tab = pl.BlockSpec((t, hd), lambda b, i: (0, 0))
    trows, tcols = table_a.shape
    assert table_b.shape == table_a.shape and trows % (batch * h) == 0
    slab = pl.BlockSpec((trows // (batch * h), tcols), lambda b, i: (b * h + i, 0))
    return pl.pallas_call(
        _moba_kernel,
        grid=(batch, h),
        in_specs=[
            pl.BlockSpec((t, hd), lambda b, i: (b, qb + i)),
            pl.BlockSpec((t, hd), lambda b, i: (b, kb + i)),
            pl.BlockSpec((t, hd), lambda b, i: (b, vb + i)),
            pl.BlockSpec((1, hd), lambda b, i: (0, 0)),
            pl.BlockSpec((1, hd), lambda b, i: (0, 0)),
            tab, tab, tab, slab, slab,
        ],
        out_specs=[pl.BlockSpec((t, hd), lambda b, i: (b, i)), slab, slab],
        out_shape=[jax.ShapeDtypeStruct((n, h * hd), jnp.bfloat16),
                   jax.ShapeDtypeStruct(table_a.shape, jnp.bfloat16),
                   jax.ShapeDtypeStruct(table_b.shape, jnp.bfloat16)],
        scratch_shapes=[
            pltpu.VMEM((t, 2 * hd), jnp.bfloat16),
            pltpu.VMEM((t, 2 * hd), jnp.bfloat16),
            pltpu.VMEM((t, hd), jnp.bfloat16),
            pltpu.VMEM((LANES, t), jnp.float32),
        ],
        compiler_params=_params("parallel", "parallel"),
        name="moba",
    )(proj, proj, proj, q_g, k_g, cos_t, sa_t, sb_t, table_a, table_b)


def _out_proj_kernel(x_ref, a_ref, b_ref, wa_ref, wb_ref, o_ref, wab_ref, wbb_ref):
    @pl.when(pl.program_id(1) == 0)
    def _():
        wab_ref[...] = wa_ref[...].astype(jnp.bfloat16)
        wbb_ref[...] = wb_ref[...].astype(jnp.bfloat16)

    o_ref[...] = x_ref[...] + _dot(a_ref[...], wab_ref[...]) + _dot(b_ref[...], wbb_ref[...])


def _out_proj(x2, o_gla, o_moba, w):
    n, d = x2.shape
    ka, kb = o_gla.shape[1], o_moba.shape[1]
    assert ka == kb and w.shape[0] == ka + kb
    tm = _tile(n, 512)
    tn = _tile(d, 1024)
    return pl.pallas_call(
        _out_proj_kernel,
        grid=(d // tn, n // tm),
        in_specs=[
            pl.BlockSpec((tm, tn), lambda j, i: (i, j)),
            pl.BlockSpec((tm, ka), lambda j, i: (i, 0)),
            pl.BlockSpec((tm, kb), lambda j, i: (i, 0)),
            pl.BlockSpec((ka, tn), lambda j, i: (0, j)),
            pl.BlockSpec((kb, tn), lambda j, i: (1, j)),
        ],
        out_specs=pl.BlockSpec((tm, tn), lambda j, i: (i, j)),
        out_shape=jax.ShapeDtypeStruct((n, d), jnp.float32),
        scratch_shapes=[pltpu.VMEM((ka, tn), jnp.bfloat16), pltpu.VMEM((kb, tn), jnp.bfloat16)],
        compiler_params=_params("parallel", "arbitrary"),
        name="out_proj",
    )(x2, o_gla, o_moba, w, w)


def _peer_q_kernel(x_ref, g_ref, w_ref, keys_ref, xn_ref, st_ref, hb_ref):
    @pl.when(pl.program_id(1) == 0)
    def _():
        x = x_ref[...]
        r = lax.rsqrt(jnp.mean(x * x, axis=-1, keepdims=True) + EPS)
        hb = ((x * r) * g_ref[...]).astype(jnp.bfloat16)
        hb_ref[...] = hb
        xn_ref[...] = hb

    q = _dot(hb_ref[...], w_ref[...])
    half = keys_ref.shape[3]
    for hh in range(keys_ref.shape[0]):
        for p in range(2):
            c0 = (2 * hh + p) * half
            st_ref[hh, p] = _dot_nt(keys_ref[hh, p].astype(jnp.bfloat16), q[:, c0:c0 + half].astype(jnp.bfloat16))


PEER_Q_HEADS_PER_STEP = 2


def _peer_q(x1, g, w_pq, keys):
    n, d = x1.shape
    h, dq, nk = PEER_HEADS, PEER_DQ, PEER_NKEYS
    hp = math.gcd(h, PEER_Q_HEADS_PER_STEP)
    tm = _tile(n, 512)
    return pl.pallas_call(
        _peer_q_kernel,
        grid=(n // tm, h // hp),
        in_specs=[
            pl.BlockSpec((tm, d), lambda i, j: (i, 0)),
            pl.BlockSpec((1, d), lambda i, j: (0, 0)),
            pl.BlockSpec((d, hp * dq), lambda i, j: (0, j)),
            pl.BlockSpec((hp, 2, nk, dq // 2), lambda i, j: (j, 0, 0, 0)),
        ],
        out_specs=[
            pl.BlockSpec((tm, d), lambda i, j: (i, 0)),
            pl.BlockSpec((hp, 2, nk, tm), lambda i, j: (j, 0, 0, i)),
        ],
        out_shape=[
            jax.ShapeDtypeStruct((n, d), jnp.bfloat16),
            jax.ShapeDtypeStruct((h, 2, nk, n), jnp.float32),
        ],
        scratch_shapes=[pltpu.VMEM((tm, d), jnp.bfloat16)],
        compiler_params=_params("parallel", "arbitrary"),
        name="peer_q",
    )(x1, g, w_pq, keys)


def _cmp_exchange(a, i, l):
    hi, lo = jnp.maximum(a[i], a[l]), jnp.minimum(a[i], a[l])
    a[i], a[l] = hi, lo


def _bitonic_merge_desc(a):
    n = len(a)
    j = n // 2
    while j >= 1:
        for i in range(n):
            if i ^ j > i:
                _cmp_exchange(a, i, i ^ j)
        j //= 2


def _top_of_sorted_lists(a):
    k = len(a)
    dist = SUBLANES // 2
    while dist >= 1:
        b = [pltpu.roll(x, SUBLANES - dist, 0) for x in a]
        a = [jnp.maximum(a[i], b[k - 1 - i]) for i in range(k)]
        _bitonic_merge_desc(a)
        dist //= 2
    return a


def _sorted_topk(s, k, out_ref):
    assert s.shape[0] == k * SUBLANES and k & (k - 1) == 0
    a = [s[r * SUBLANES:(r + 1) * SUBLANES, :] for r in range(k)]
    size = 2
    while size <= k:
        j = size // 2
        while j >= 1:
            for i in range(k):
                l = i ^ j
                if l > i:
                    if (i & size) == 0:
                        _cmp_exchange(a, i, l)
                    else:
                        _cmp_exchange(a, l, i)
            j //= 2
        size *= 2
    a = _top_of_sorted_lists(a)
    for r in range(k):
        out_ref[r:r + 1, :] = a[r][0:1, :]
    return a[k - 1][0:1, :]


def _peer_topk_kernel(st_ref, s1m_ref, e1_ref, s2m_ref, e2n_ref, tau_ref, v1_ref, v2_ref):
    k = PEER_TOPK

    def one_head(hh, carry):
        s1 = st_ref[hh, 0]
        s2 = st_ref[hh, 1]
        th1 = _sorted_topk(s1, k, v1_ref)
        th2 = _sorted_topk(s2, k, v2_ref)
        s1m_ref[hh] = jnp.where(s1 >= th1, s1, _NEG_INF)
        s2m_ref[hh] = jnp.where(s2 >= th2, s2, _NEG_INF)
        v1 = v1_ref[...]
        v2 = v2_ref[...]
        m1 = v1[0:1, :]
        m2 = v2[0:1, :]
        e1_ref[hh] = jnp.exp(s1 - m1)
        e2 = jnp.exp(s2 - m2)
        ev1 = jnp.exp(v1 - m1)
        ev2 = jnp.exp(v2 - m2)

        cand = [v1 + v2[b:b + 1, :] for b in range(k)]
        lo = [c[0:SUBLANES, :] for c in cand]
        hi = [c[SUBLANES:2 * SUBLANES, :] for c in cand]
        top = [jnp.maximum(lo[i], hi[k - 1 - i]) for i in range(k)]
        _bitonic_merge_desc(top)
        tau = _top_of_sorted_lists(top)[k - 1][0:1, :]
        zsum = jnp.zeros_like(tau)
        for b in range(k):
            picked = jnp.where(cand[b] >= tau, ev1 * ev2[b:b + 1, :], 0.0)
            zsum = zsum + jnp.sum(picked, axis=0, keepdims=True)
        e2n_ref[hh] = e2 * (1.0 / zsum)
        tau_ref[hh] = tau
        return carry

    lax.fori_loop(0, st_ref.shape[0], one_head, 0)


def _peer_topk(st):
    h, _, nk, n = st.shape
    tl = _tile(n, 256)
    big = pl.BlockSpec((h, nk, tl), lambda i: (0, 0, i))
    shp = jax.ShapeDtypeStruct((h, nk, n), jnp.float32)
    return pl.pallas_call(
        _peer_topk_kernel,
        grid=(n // tl,),
        in_specs=[pl.BlockSpec((h, 2, nk, tl), lambda i: (0, 0, 0, i))],
        out_specs=[big, big, big, big, pl.BlockSpec((h, 1, tl), lambda i: (0, 0, i))],
        out_shape=[shp, shp, shp, shp, jax.ShapeDtypeStruct((h, 1, n), jnp.float32)],
        scratch_shapes=[pltpu.VMEM((PEER_TOPK, tl), jnp.float32),
                        pltpu.VMEM((PEER_TOPK, tl), jnp.float32)],
        compiler_params=_params("parallel"),
        name="peer_topk",
    )(st)


def _peer_ffn_kernel(x1_ref, xn_ref, u_ref, v_ref, s1m_ref, e1_ref, s2m_ref, e2n_ref, tau_ref, o_ref,
                     *, rows_per_step):
    e = pl.program_id(1)
    nh = s2m_ref.shape[0]
    nk = s2m_ref.shape[1]
    rblk = s1m_ref.shape[1]

    @pl.when(e == 0)
    def _():
        o_ref[...] = x1_ref[...]

    tm, d = xn_ref.shape
    base = (e % (rblk // rows_per_step)) * rows_per_step
    slabs = []
    zeros = []
    half_word = jnp.uint32(16)
    for r in range(rows_per_step):
        i8 = base + r
        acc = jnp.zeros((nk, tm), jnp.float32)
        for h in range(nh):
            ssum = s1m_ref[h, pl.ds(i8, 1), :] + s2m_ref[h]
            w = e1_ref[h, pl.ds(i8, 1), :] * e2n_ref[h]
            acc = acc + jnp.where(ssum >= tau_ref[h], w, 0.0)
        slabs.append(acc.T)
        bits = lax.bitcast_convert_type(acc, jnp.uint32)
        zf = lax.bitcast_convert_type(
            lax.shift_right_logical(lax.shift_right_logical(bits, half_word), half_word), jnp.float32)
        for c in range(tm // LANES):
            for g in range(nk // SUBLANES):
                blk = zf[g * SUBLANES:(g + 1) * SUBLANES, c * LANES:(c + 1) * LANES]
                zeros.append(jnp.concatenate([blk, blk], axis=0).astype(jnp.bfloat16))
    wsel = jnp.concatenate(slabs, axis=1) if len(slabs) > 1 else slabs[0]

    rb = 2 * SUBLANES
    kw = 2 * LANES
    n_pairs = (tm // rb) * (d // kw)
    cols = []
    for kb in range(d // kw):
        rows = []
        for mb in range(tm // rb):
            zb = zeros[((kb * (tm // rb) + mb) * len(zeros)) // n_pairs]
            rows.append(xn_ref[mb * rb:(mb + 1) * rb, kb * kw:(kb + 1) * kw] + jnp.concatenate([zb, zb], axis=1))
        cols.append(jnp.concatenate(rows, axis=0))
    hmat = _dot_nt(jnp.concatenate(cols, axis=1), u_ref[...])
    act = 0.5 * hmat * (1.0 + lax.erf(hmat * (1.0 / math.sqrt(2.0))))
    z = jnp.where(wsel != 0.0, wsel * act, 0.0).astype(jnp.bfloat16)
    o_ref[...] += _dot(z, v_ref[...])


def _peer_ffn(x1, xn, u_b, v_b, s1m, e1, s2m, e2n, tau):
    n, d = x1.shape
    ne = u_b.shape[0]
    nh, nk, _ = s1m.shape
    tm = _tile(n, 512)
    te = _tile(ne, 512)
    rows_per_step = te // nk
    rblk = max(SUBLANES, rows_per_step)
    steps_per_rblk = rblk // rows_per_step
    once = dict(pipeline_mode=pl.Buffered(1))
    return pl.pallas_call(
        functools.partial(_peer_ffn_kernel, rows_per_step=rows_per_step),
        grid=(n // tm, ne // te),
        in_specs=[
            pl.BlockSpec((tm, d), lambda i, j: (i, 0), **once),
            pl.BlockSpec((tm, d), lambda i, j: (i, 0)),
            pl.BlockSpec((te, d), lambda i, j: (j, 0)),
            pl.BlockSpec((te, d), lambda i, j: (j, 0)),
            pl.BlockSpec((nh, rblk, tm), lambda i, j: (0, j // steps_per_rblk, i)),
            pl.BlockSpec((nh, rblk, tm), lambda i, j: (0, j // steps_per_rblk, i)),
            pl.BlockSpec((nh, nk, tm), lambda i, j: (0, 0, i)),
            pl.BlockSpec((nh, nk, tm), lambda i, j: (0, 0, i)),
            pl.BlockSpec((nh, 1, tm), lambda i, j: (0, 0, i)),
        ],
        out_specs=pl.BlockSpec((tm, d), lambda i, j: (i, 0)),
        out_shape=jax.ShapeDtypeStruct((n, d), jnp.float32),
        compiler_params=_params("parallel", "arbitrary"),
        name="peer_ffn",
    )(x1, xn, u_b, v_b, s1m, e1, s2m, e2n, tau)


def _rope_tables(t, hd):
    half = ROPE_DIMS // 2
    inv = ROPE_THETA ** (-jnp.arange(half, dtype=jnp.float32) / half)
    ang = jnp.arange(t).astype(jnp.float32)[:, None] * inv[None, :]
    cos, sin = jnp.cos(ang), jnp.sin(ang)
    zeros = jnp.zeros((t, hd - 2 * half), jnp.float32)
    z16 = jnp.zeros((t, half), jnp.float32)
    cos_t = jnp.concatenate([cos, cos, jnp.ones_like(zeros)], axis=1)
    sa_t = jnp.concatenate([z16, sin, zeros], axis=1)
    sb_t = jnp.concatenate([-sin, z16, zeros], axis=1)
    return cos_t, sa_t, sb_t


def kernel(x, norm_mix_g, w_in, w_gate_up, b_gate, gla_norm_g, q_norm_g, k_norm_g, w_out, norm_ffn_g,
           peer_wq, peer_keys, peer_u, peer_v):
    batch, t, d = x.shape
    n = batch * t
    bf16 = jnp.bfloat16
    gla_qk_w = GLA_HEADS * GLA_DK
    gla_v_w = GLA_HEADS * GLA_DV
    lr0 = 2 * gla_qk_w + 2 * gla_v_w
    cos_t, sa_t, sb_t = _rope_tables(t, MOBA_HD)

    xcur = x.reshape(n, d)
    for l in range(w_in.shape[0]):
        wt = jnp.swapaxes(w_in[l], 0, 1)
        wlr_t = jnp.pad(wt[lr0:lr0 + GLA_GATE_RANK], ((0, LANES - GLA_GATE_RANK), (0, 0))).astype(bf16)
        wup_pad = jnp.pad(w_gate_up[l], ((0, LANES - GLA_GATE_RANK), (0, 0)))

        h, glr = _norm_lowrank(xcur, norm_mix_g[l][None, :], wlr_t)
        proj = _in_proj(h, wt, lr0, GLA_GATE_RANK)
        o_gla = _gla(proj, glr, wup_pad, b_gate[l][None, :], gla_norm_g[l][None, :], batch, t)
        o_moba, u_b, v_b = _moba(proj, q_norm_g[l][None, :], k_norm_g[l][None, :], cos_t, sa_t, sb_t, batch, t, lr0,
                                 peer_u[l], peer_v[l])
        x1 = _out_proj(xcur, o_gla, o_moba, w_out[l])

        xn, st = _peer_q(x1, norm_ffn_g[l][None, :], peer_wq[l].astype(bf16), peer_keys[l])
        s1m, e1, s2m, e2n, tau = _peer_topk(st)
        xcur = _peer_ffn(x1, xn, u_b, v_b, s1m, e1, s2m, e2n, tau)
    return xcur.reshape(batch, t, d)
```

```python
import functools
import math

import jax
import jax.numpy as jnp
from jax import lax
from jax.experimental import pallas as pl
from jax.experimental.pallas import tpu as pltpu

GLA_HEADS = 8
GLA_DK = 128
GLA_DV = 256
GLA_GATE_RANK = 16
GLA_GATE_NORMALIZER = 16.0
GLA_CHUNK = 64
MOBA_HEADS = 16
MOBA_HD = 128
MOBA_BLOCK = 256
MOBA_TOPK = 3
ROPE_THETA = 500000.0
ROPE_DIMS = MOBA_HD // 4
PEER_HEADS = 8
PEER_NKEYS = 128
PEER_DQ = 256
PEER_TOPK = 16
EPS = 1e-6

LANES = 128
SUBLANES = 8
VMEM_LIMIT_BYTES = 60 * 1024 * 1024
_HI = lax.Precision.HIGHEST
_NEG_INF = float("-inf")


def _dot(a, b, precision=None):
    return lax.dot_general(a, b, (((1,), (0,)), ((), ())), precision=precision,
                           preferred_element_type=jnp.float32)


def _dot_nt(a, b, precision=None):
    return lax.dot_general(a, b, (((1,), (1,)), ((), ())), precision=precision,
                           preferred_element_type=jnp.float32)


def _dot_tn(a, b, precision=None):
    return lax.dot_general(a, b, (((0,), (0,)), ((), ())), precision=precision,
                           preferred_element_type=jnp.float32)


def _tile(n, pref):
    t = min(n, pref)
    while n % t:
        t //= 2
    return t


def _params(*sem, flags=None):
    return pltpu.CompilerParams(dimension_semantics=sem, vmem_limit_bytes=VMEM_LIMIT_BYTES, flags=flags)


def _norm_kernel(x_ref, g_ref, wlrt_ref, h_ref, olr_ref):
    x = x_ref[...]
    r = lax.rsqrt(jnp.mean(x * x, axis=-1, keepdims=True) + EPS)
    hb = ((x * r) * g_ref[...]).astype(jnp.bfloat16)
    h_ref[...] = hb
    olr_ref[...] = _dot_nt(hb, wlrt_ref[...])


def _norm_lowrank(x2, g, wlr_t):
    n, d = x2.shape
    tm = _tile(n, 512)
    return pl.pallas_call(
        _norm_kernel,
        grid=(n // tm,),
        in_specs=[
            pl.BlockSpec((tm, d), lambda i: (i, 0)),
            pl.BlockSpec((1, d), lambda i: (0, 0)),
            pl.BlockSpec((LANES, d), lambda i: (0, 0)),
        ],
        out_specs=[
            pl.BlockSpec((tm, d), lambda i: (i, 0)),
            pl.BlockSpec((tm, LANES), lambda i: (i, 0)),
        ],
        out_shape=[
            jax.ShapeDtypeStruct((n, d), jnp.bfloat16),
            jax.ShapeDtypeStruct((n, LANES), jnp.float32),
        ],
        compiler_params=_params("parallel"),
        name="norm_mix",
    )(x2, g, wlr_t)


def _in_proj_kernel(h_ref, wt_ref, o_ref, wb_ref):
    @pl.when(pl.program_id(1) == 0)
    def _():
        wb_ref[...] = wt_ref[...].astype(jnp.bfloat16)

    o_ref[...] = _dot_nt(h_ref[...], wb_ref[...])


def _in_proj(h, wt, split, skip):
    n, d = h.shape
    cols = wt.shape[0] - skip
    tm = _tile(n, 512)
    tn = _tile(math.gcd(split, cols - split), 1024)
    return pl.pallas_call(
        _in_proj_kernel,
        grid=(cols // tn, n // tm),
        in_specs=[
            pl.BlockSpec((tm, d), lambda j, i: (i, 0)),
            pl.BlockSpec((pl.Element(tn), pl.Element(d)),
                         lambda j, i: (pl.multiple_of(j * tn + jnp.where(j * tn >= split, skip, 0), skip), 0)),
        ],
        out_specs=pl.BlockSpec((tm, tn), lambda j, i: (i, j)),
        out_shape=jax.ShapeDtypeStruct((n, cols), jnp.float32),
        scratch_shapes=[pltpu.VMEM((tn, d), jnp.bfloat16)],
        compiler_params=_params("parallel", "arbitrary"),
        name="in_proj",
    )(h, wt)


GLA_GROUP = 256


def _gla_kernel(q_ref, k_ref, v_ref, go_ref, glr_ref, wup_ref, bg_ref, ng_ref, o_ref,
                qd_ref, ki_ref, ks_ref, vb_ref, oacc_ref, dec_ref, upd_ref, sprev_ref):
    t, dk = q_ref.shape
    dv = v_ref.shape[1]
    c = GLA_CHUNK
    n = t // c
    grp = min(GLA_GROUP, t)
    scale = dk ** -0.5
    bf16 = jnp.bfloat16

    z = _dot(glr_ref[...], wup_ref[...], _HI) + bg_ref[...]
    log_a = (jnp.minimum(z, 0.0) - jnp.log(1.0 + jnp.exp(-jnp.abs(z)))) / GLA_GATE_NORMALIZER
    row = lax.broadcasted_iota(jnp.int32, (grp, grp), 0)
    col = lax.broadcasted_iota(jnp.int32, (grp, grp), 1)
    same_chunk_causal = (row >= col) & ((row // c) == (col // c))
    tril = same_chunk_causal.astype(jnp.float32)
    bc = jnp.concatenate([_dot(tril, log_a[g * grp:(g + 1) * grp, :], _HI) for g in range(t // grp)], axis=0)
    bc3 = bc.reshape(n, c, dk)
    bl3 = bc3[:, c - 1:c, :]
    k3 = k_ref[...].reshape(n, c, dk)
    qd_ref[...] = (q_ref[...] * scale * jnp.exp(bc)).astype(bf16)
    ki_ref[...] = (k_ref[...] * jnp.exp(-bc)).astype(bf16)
    ks_ref[...] = (k3 * jnp.exp(bl3 - bc3)).reshape(t, dk).astype(bf16)
    dec_ref[...] = jnp.exp(bl3)
    vb_ref[...] = v_ref[...].astype(bf16)

    for g in range(t // grp):
        rows = pl.ds(g * grp, grp)
        att = jnp.where(same_chunk_causal, _dot_nt(qd_ref[rows, :], ki_ref[rows, :]), 0.0)
        oacc_ref[rows, :] = _dot(att.astype(bf16), vb_ref[rows, :])

    cpg_u = grp // c
    erow = lax.broadcasted_iota(jnp.int32, (grp, cpg_u * dk), 0) // c
    ecol = lax.broadcasted_iota(jnp.int32, (grp, cpg_u * dk), 1) // dk
    own_chunk = erow == ecol
    for g in range(t // grp):
        rows = pl.ds(g * grp, grp)
        ks_g = ks_ref[rows, :]
        ks_exp = jnp.where(own_chunk, jnp.concatenate([ks_g] * cpg_u, axis=1), jnp.zeros((), bf16))
        res = _dot_tn(vb_ref[rows, :], ks_exp)
        for i in range(cpg_u):
            upd_ref[g * cpg_u + i] = res[:, i * dk:(i + 1) * dk]

    def scan_body(ci, st):
        sprev_ref[ci] = st.astype(bf16)
        return dec_ref[ci] * st + upd_ref[ci]

    lax.fori_loop(0, n, scan_body, jnp.zeros((dv, dk), jnp.float32), unroll=2)

    cpg = grp // c
    for g in range(t // grp):
        rows = pl.ds(g * grp, grp)
        states = sprev_ref[g * cpg:(g + 1) * cpg].reshape(cpg * dv, dk)
        res = _dot_nt(qd_ref[rows, :], states)
        oacc_ref[rows, :] += jnp.concatenate(
            [res[i * c:(i + 1) * c, i * dv:(i + 1) * dv] for i in range(cpg)], axis=0)

    o = oacc_ref[...]
    r = lax.rsqrt(jnp.mean(o * o, axis=-1, keepdims=True) + EPS)
    go = go_ref[...]
    o_ref[...] = (((o * r) * ng_ref[...]) * (go * jax.nn.sigmoid(go))).astype(o_ref.dtype)


def _gla(proj, glr, wup_pad, b_gate, norm_g, batch, t):
    n = proj.shape[0]
    h, dk, dv = GLA_HEADS, GLA_DK, GLA_DV
    nc = t // GLA_CHUNK
    qk_w = h * dk
    kb = qk_w // dk
    vb = (2 * qk_w) // dv
    gb = (2 * qk_w + h * dv) // dv
    return pl.pallas_call(
        _gla_kernel,
        grid=(batch, h),
        in_specs=[
            pl.BlockSpec((t, dk), lambda b, i: (b, i)),
            pl.BlockSpec((t, dk), lambda b, i: (b, kb + i)),
            pl.BlockSpec((t, dv), lambda b, i: (b, vb + i)),
            pl.BlockSpec((t, dv), lambda b, i: (b, gb + i)),
            pl.BlockSpec((t, LANES), lambda b, i: (b, 0)),
            pl.BlockSpec((LANES, dk), lambda b, i: (0, i)),
            pl.BlockSpec((1, dk), lambda b, i: (0, i)),
            pl.BlockSpec((1, dv), lambda b, i: (0, 0)),
        ],
        out_specs=pl.BlockSpec((t, dv), lambda b, i: (b, i)),
        out_shape=jax.ShapeDtypeStruct((n, h * dv), jnp.bfloat16),
        scratch_shapes=[
            pltpu.VMEM((t, dk), jnp.bfloat16),
            pltpu.VMEM((t, dk), jnp.bfloat16),
            pltpu.VMEM((t, dk), jnp.bfloat16),
            pltpu.VMEM((t, dv), jnp.bfloat16),
            pltpu.VMEM((t, dv), jnp.float32),
            pltpu.VMEM((nc, 1, dk), jnp.float32),
            pltpu.VMEM((nc, dv, dk), jnp.float32),
            pltpu.VMEM((nc, dv, dk), jnp.bfloat16),
        ],
        compiler_params=_params("parallel", "parallel"),
        name="gla",
    )(proj, proj, proj, proj, glr, wup_pad, b_gate, norm_g)


MASK_VALUE = -1e30
LOG2_E = math.log2(math.e)


def _moba_kernel(q_ref, k_ref, v_ref, qg_ref, kg_ref, cos_ref, sa_ref, sb_ref, ta_ref, tb_ref,
                 o_ref, tab_ref, tbb_ref, qa_ref, ka_ref, vb_ref, bias_ref):
    t, hd = q_ref.shape
    blk = MOBA_BLOCK
    nb = t // blk
    half = ROPE_DIMS // 2
    bf16 = jnp.bfloat16

    tab_ref[...] = ta_ref[...].astype(bf16)
    tbb_ref[...] = tb_ref[...].astype(bf16)

    def prep(x, g, scale):
        r = lax.rsqrt(jnp.mean(x * x, axis=-1, keepdims=True) + EPS)
        xn = (x * r) * g
        y = (xn * cos_ref[...] + pltpu.roll(xn, half, 1) * sa_ref[...]
             + pltpu.roll(xn, hd - half, 1) * sb_ref[...])
        return y * scale

    qs = prep(q_ref[...], qg_ref[...], hd ** -0.5)
    qh = qs * LOG2_E
    kh = prep(k_ref[...], kg_ref[...], 1.0)
    vb_ref[...] = v_ref[...].astype(bf16)

    km = jnp.sum(kh.reshape(nb, blk, hd), axis=1) * (1.0 / blk)
    gate_t = _dot_nt(km, qs, _HI)
    kblk = lax.broadcasted_iota(jnp.int32, (nb, t), 0)
    qblk = lax.broadcasted_iota(jnp.int32, (nb, t), 1) // blk
    past = kblk < qblk
    gm = jnp.where(past, gate_t, _NEG_INF)
    rank = jnp.zeros((nb, t), jnp.float32)
    for n2 in range(nb):
        other = gm[n2:n2 + 1, :]
        beats = (other > gm) | ((other == gm) & (kblk > n2))
        rank = rank + beats.astype(jnp.float32)
    allowed = (past & (rank < float(MOBA_TOPK))) | (kblk == qblk)
    bias_ref[...] = jnp.zeros_like(bias_ref)
    bias_ref[0:nb, :] = jnp.where(allowed, 0.0, MASK_VALUE)

    qa_ref[:, 0:hd] = qh.astype(bf16)
    qa_ref[:, hd:2 * hd] = bias_ref[...].T.astype(bf16)
    ka_ref[:, 0:hd] = kh.astype(bf16)
    lane = lax.broadcasted_iota(jnp.int32, (t, hd), 1)
    rblk = lax.broadcasted_iota(jnp.int32, (t, hd), 0) // blk
    ka_ref[:, hd:2 * hd] = (lane == rblk).astype(bf16)

    lrow = lax.broadcasted_iota(jnp.int32, (blk, blk), 0)
    lcol = lax.broadcasted_iota(jnp.int32, (blk, blk), 1)
    causal = lrow >= lcol
    for j in range(nb):
        rows = pl.ds(j * blk, blk)
        nkeys = (j + 1) * blk
        s = _dot_nt(qa_ref[rows, :], ka_ref[0:nkeys, :])
        own = jnp.where(causal, s[:, j * blk:nkeys], MASK_VALUE)
        s = own if j == 0 else jnp.concatenate([s[:, :j * blk], own], axis=1)
        m = jnp.max(s, axis=-1, keepdims=True)
        p = jnp.exp2(s - m)
        l = jnp.sum(p, axis=-1, keepdims=True)
        o = _dot(p.astype(bf16), vb_ref[0:nkeys, :])
        o_ref[rows, :] = (o * (1.0 / l)).astype(o_ref.dtype)


def _moba(proj, q_g, k_g, cos_t, sa_t, sb_t, batch, t, col0, table_a, table_b):
    n = proj.shape[0]
    h, hd = MOBA_HEADS, MOBA_HD
    qb = col0 // hd
    kb = qb + h
    vb = kb + h
    tab = pl.BlockSpec((t, hd), lambda b, i: (0, 0))
    trows, tcols = table_a.shape
    assert table_b.shape == table_a.shape and trows % (batch * h) == 0
    slab = pl.BlockSpec((trows // (batch * h), tcols), lambda b, i: (b * h + i, 0))
    return pl.pallas_call(
        _moba_kernel,
        grid=(batch, h),
        in_specs=[
            pl.BlockSpec((t, hd), lambda b, i: (b, qb + i)),
            pl.BlockSpec((t, hd), lambda b, i: (b, kb + i)),
            pl.BlockSpec((t, hd), lambda b, i: (b, vb + i)),
            pl.BlockSpec((1, hd), lambda b, i: (0, 0)),
            pl.BlockSpec((1, hd), lambda b, i: (0, 0)),
            tab, tab, tab, slab, slab,
        ],
        out_specs=[pl.BlockSpec((t, hd), lambda b, i: (b, i)), slab, slab],
        out_shape=[jax.ShapeDtypeStruct((n, h * hd), jnp.bfloat16),
                   jax.ShapeDtypeStruct(table_a.shape, jnp.bfloat16),
                   jax.ShapeDtypeStruct(table_b.shape, jnp.bfloat16)],
        scratch_shapes=[
            pltpu.VMEM((t, 2 * hd), jnp.bfloat16),
            pltpu.VMEM((t, 2 * hd), jnp.bfloat16),
            pltpu.VMEM((t, hd), jnp.bfloat16),
            pltpu.VMEM((LANES, t), jnp.float32),
        ],
        compiler_params=_params("parallel", "parallel"),
        name="moba",
    )(proj, proj, proj, q_g, k_g, cos_t, sa_t, sb_t, table_a, table_b)


def _out_proj_kernel(x_ref, a_ref, b_ref, wa_ref, wb_ref, o_ref, wab_ref, wbb_ref):
    @pl.when(pl.program_id(1) == 0)
    def _():
        wab_ref[...] = wa_ref[...].astype(jnp.bfloat16)
        wbb_ref[...] = wb_ref[...].astype(jnp.bfloat16)

    o_ref[...] = x_ref[...] + _dot(a_ref[...], wab_ref[...]) + _dot(b_ref[...], wbb_ref[...])


def _out_proj(x2, o_gla, o_moba, w):
    n, d = x2.shape
    ka, kb = o_gla.shape[1], o_moba.shape[1]
    assert ka == kb and w.shape[0] == ka + kb
    tm = _tile(n, 512)
    tn = _tile(d, 1024)
    return pl.pallas_call(
        _out_proj_kernel,
        grid=(d // tn, n // tm),
        in_specs=[
            pl.BlockSpec((tm, tn), lambda j, i: (i, j)),
            pl.BlockSpec((tm, ka), lambda j, i: (i, 0)),
            pl.BlockSpec((tm, kb), lambda j, i: (i, 0)),
            pl.BlockSpec((ka, tn), lambda j, i: (0, j)),
            pl.BlockSpec((kb, tn), lambda j, i: (1, j)),
        ],
        out_specs=pl.BlockSpec((tm, tn), lambda j, i: (i, j)),
        out_shape=jax.ShapeDtypeStruct((n, d), jnp.float32),
        scratch_shapes=[pltpu.VMEM((ka, tn), jnp.bfloat16), pltpu.VMEM((kb, tn), jnp.bfloat16)],
        compiler_params=_params("parallel", "arbitrary"),
        name="out_proj",
    )(x2, o_gla, o_moba, w, w)


def _peer_q_kernel(x_ref, g_ref, w_ref, keys_ref, xn_ref, st_ref, hb_ref):
    @pl.when(pl.program_id(1) == 0)
    def _():
        x = x_ref[...]
        r = lax.rsqrt(jnp.mean(x * x, axis=-1, keepdims=True) + EPS)
        hb = ((x * r) * g_ref[...]).astype(jnp.bfloat16)
        hb_ref[...] = hb
        xn_ref[...] = hb

    q = _dot(hb_ref[...], w_ref[...])
    half = keys_ref.shape[3]
    for hh in range(keys_ref.shape[0]):
        for p in range(2):
            c0 = (2 * hh + p) * half
            st_ref[hh, p] = _dot_nt(keys_ref[hh, p].astype(jnp.bfloat16), q[:, c0:c0 + half].astype(jnp.bfloat16))


PEER_Q_HEADS_PER_STEP = 2


def _peer_q(x1, g, w_pq, keys):
    n, d = x1.shape
    h, dq, nk = PEER_HEADS, PEER_DQ, PEER_NKEYS
    hp = math.gcd(h, PEER_Q_HEADS_PER_STEP)
    tm = _tile(n, 512)
    return pl.pallas_call(
        _peer_q_kernel,
        grid=(n // tm, h // hp),
        in_specs=[
            pl.BlockSpec((tm, d), lambda i, j: (i, 0)),
            pl.BlockSpec((1, d), lambda i, j: (0, 0)),
            pl.BlockSpec((d, hp * dq), lambda i, j: (0, j)),
            pl.BlockSpec((hp, 2, nk, dq // 2), lambda i, j: (j, 0, 0, 0)),
        ],
        out_specs=[
            pl.BlockSpec((tm, d), lambda i, j: (i, 0)),
            pl.BlockSpec((hp, 2, nk, tm), lambda i, j: (j, 0, 0, i)),
        ],
        out_shape=[
            jax.ShapeDtypeStruct((n, d), jnp.bfloat16),
            jax.ShapeDtypeStruct((h, 2, nk, n), jnp.float32),
        ],
        scratch_shapes=[pltpu.VMEM((tm, d), jnp.bfloat16)],
        compiler_params=_params("parallel", "arbitrary"),
        name="peer_q",
    )(x1, g, w_pq, keys)


def _cmp_exchange(a, i, l):
    hi, lo = jnp.maximum(a[i], a[l]), jnp.minimum(a[i], a[l])
    a[i], a[l] = hi, lo


def _bitonic_merge_desc(a):
    n = len(a)
    j = n // 2
    while j >= 1:
        for i in range(n):
            if i ^ j > i:
                _cmp_exchange(a, i, i ^ j)
        j //= 2


def _top_of_sorted_lists(a):
    k = len(a)
    dist = SUBLANES // 2
    while dist >= 1:
        b = [pltpu.roll(x, SUBLANES - dist, 0) for x in a]
        a = [jnp.maximum(a[i], b[k - 1 - i]) for i in range(k)]
        _bitonic_merge_desc(a)
        dist //= 2
    return a


def _sorted_topk(s, k, out_ref):
    assert s.shape[0] == k * SUBLANES and k & (k - 1) == 0
    a = [s[r * SUBLANES:(r + 1) * SUBLANES, :] for r in range(k)]
    size = 2
    while size <= k:
        j = size // 2
        while j >= 1:
            for i in range(k):
                l = i ^ j
                if l > i:
                    if (i & size) == 0:
                        _cmp_exchange(a, i, l)
                    else:
                        _cmp_exchange(a, l, i)
            j //= 2
        size *= 2
    a = _top_of_sorted_lists(a)
    for r in range(k):
        out_ref[r:r + 1, :] = a[r][0:1, :]
    return a[k - 1][0:1, :]


def _peer_topk_kernel(st_ref, s1m_ref, e1_ref, s2m_ref, e2n_ref, tau_ref, v1_ref, v2_ref):
    k = PEER_TOPK

    def one_head(hh, carry):
        s1 = st_ref[hh, 0]
        s2 = st_ref[hh, 1]
        th1 = _sorted_topk(s1, k, v1_ref)
        th2 = _sorted_topk(s2, k, v2_ref)
        s1m_ref[hh] = jnp.where(s1 >= th1, s1, _NEG_INF)
        s2m_ref[hh] = jnp.where(s2 >= th2, s2, _NEG_INF)
        v1 = v1_ref[...]
        v2 = v2_ref[...]
        m1 = v1[0:1, :]
        m2 = v2[0:1, :]
        e1_ref[hh] = jnp.exp(s1 - m1)
        e2 = jnp.exp(s2 - m2)
        ev1 = jnp.exp(v1 - m1)
        ev2 = jnp.exp(v2 - m2)

        cand = [v1 + v2[b:b + 1, :] for b in range(k)]
        lo = [c[0:SUBLANES, :] for c in cand]
        hi = [c[SUBLANES:2 * SUBLANES, :] for c in cand]
        top = [jnp.maximum(lo[i], hi[k - 1 - i]) for i in range(k)]
        _bitonic_merge_desc(top)
        tau = _top_of_sorted_lists(top)[k - 1][0:1, :]
        zsum = jnp.zeros_like(tau)
        for b in range(k):
            picked = jnp.where(cand[b] >= tau, ev1 * ev2[b:b + 1, :], 0.0)
            zsum = zsum + jnp.sum(picked, axis=0, keepdims=True)
        e2n_ref[hh] = e2 * (1.0 / zsum)
        tau_ref[hh] = tau
        return carry

    lax.fori_loop(0, st_ref.shape[0], one_head, 0)


def _peer_topk(st):
    h, _, nk, n = st.shape
    tl = _tile(n, 256)
    big = pl.BlockSpec((h, nk, tl), lambda i: (0, 0, i))
    shp = jax.ShapeDtypeStruct((h, nk, n), jnp.float32)
    return pl.pallas_call(
        _peer_topk_kernel,
        grid=(n // tl,),
        in_specs=[pl.BlockSpec((h, 2, nk, tl), lambda i: (0, 0, 0, i))],
        out_specs=[big, big, big, big, pl.BlockSpec((h, 1, tl), lambda i: (0, 0, i))],
        out_shape=[shp, shp, shp, shp, jax.ShapeDtypeStruct((h, 1, n), jnp.float32)],
        scratch_shapes=[pltpu.VMEM((PEER_TOPK, tl), jnp.float32),
                        pltpu.VMEM((PEER_TOPK, tl), jnp.float32)],
        compiler_params=_params("parallel"),
        name="peer_topk",
    )(st)


def _peer_ffn_kernel(x1_ref, xn_ref, u_ref, v_ref, s1m_ref, e1_ref, s2m_ref, e2n_ref, tau_ref, o_ref,
                     *, rows_per_step):
    e = pl.program_id(1)
    nh = s2m_ref.shape[0]
    nk = s2m_ref.shape[1]
    rblk = s1m_ref.shape[1]

    @pl.when(e == 0)
    def _():
        o_ref[...] = x1_ref[...]

    tm, d = xn_ref.shape
    base = (e % (rblk // rows_per_step)) * rows_per_step
    slabs = []
    zeros = []
    half_word = jnp.uint32(16)
    for r in range(rows_per_step):
        i8 = base + r
        acc = jnp.zeros((nk, tm), jnp.float32)
        for h in range(nh):
            ssum = s1m_ref[h, pl.ds(i8, 1), :] + s2m_ref[h]
            w = e1_ref[h, pl.ds(i8, 1), :] * e2n_ref[h]
            acc = acc + jnp.where(ssum >= tau_ref[h], w, 0.0)
        slabs.append(acc.T)
        bits = lax.bitcast_convert_type(acc, jnp.uint32)
        zf = lax.bitcast_convert_type(
            lax.shift_right_logical(lax.shift_right_logical(bits, half_word), half_word), jnp.float32)
        for c in range(tm // LANES):
            for g in range(nk // SUBLANES):
                blk = zf[g * SUBLANES:(g + 1) * SUBLANES, c * LANES:(c + 1) * LANES]
                zeros.append(jnp.concatenate([blk, blk], axis=0).astype(jnp.bfloat16))
    wsel = jnp.concatenate(slabs, axis=1) if len(slabs) > 1 else slabs[0]

    rb = 2 * SUBLANES
    kw = 2 * LANES
    n_pairs = (tm // rb) * (d // kw)
    cols = []
    for kb in range(d // kw):
        rows = []
        for mb in range(tm // rb):
            zb = zeros[((kb * (tm // rb) + mb) * len(zeros)) // n_pairs]
            rows.append(xn_ref[mb * rb:(mb + 1) * rb, kb * kw:(kb + 1) * kw] + jnp.concatenate([zb, zb], axis=1))
        cols.append(jnp.concatenate(rows, axis=0))
    hmat = _dot_nt(jnp.concatenate(cols, axis=1), u_ref[...])
    act = 0.5 * hmat * (1.0 + lax.erf(hmat * (1.0 / math.sqrt(2.0))))
    z = jnp.where(wsel != 0.0, wsel * act, 0.0).astype(jnp.bfloat16)
    o_ref[...] += _dot(z, v_ref[...])


def _peer_ffn(x1, xn, u_b, v_b, s1m, e1, s2m, e2n, tau):
    n, d = x1.shape
    ne = u_b.shape[0]
    nh, nk, _ = s1m.shape
    tm = _tile(n, 512)
    te = _tile(ne, 512)
    rows_per_step = te // nk
    rblk = max(SUBLANES, rows_per_step)
    steps_per_rblk = rblk // rows_per_step
    once = dict(pipeline_mode=pl.Buffered(1))
    return pl.pallas_call(
        functools.partial(_peer_ffn_kernel, rows_per_step=rows_per_step),
        grid=(n // tm, ne // te),
        in_specs=[
            pl.BlockSpec((tm, d), lambda i, j: (i, 0), **once),
            pl.BlockSpec((tm, d), lambda i, j: (i, 0)),
            pl.BlockSpec((te, d), lambda i, j: (j, 0)),
            pl.BlockSpec((te, d), lambda i, j: (j, 0)),
            pl.BlockSpec((nh, rblk, tm), lambda i, j: (0, j // steps_per_rblk, i)),
            pl.BlockSpec((nh, rblk, tm), lambda i, j: (0, j // steps_per_rblk, i)),
            pl.BlockSpec((nh, nk, tm), lambda i, j: (0, 0, i)),
            pl.BlockSpec((nh, nk, tm), lambda i, j: (0, 0, i)),
            pl.BlockSpec((nh, 1, tm), lambda i, j: (0, 0, i)),
        ],
        out_specs=pl.BlockSpec((tm, d), lambda i, j: (i, 0)),
        out_shape=jax.ShapeDtypeStruct((n, d), jnp.float32),
        compiler_params=_params("parallel", "arbitrary"),
        name="peer_ffn",
    )(x1, xn, u_b, v_b, s1m, e1, s2m, e2n, tau)


def _rope_tables(t, hd):
    half = ROPE_DIMS // 2
    inv = ROPE_THETA ** (-jnp.arange(half, dtype=jnp.float32) / half)
    ang = jnp.arange(t).astype(jnp.float32)[:, None] * inv[None, :]
    cos, sin = jnp.cos(ang), jnp.sin(ang)
    zeros = jnp.zeros((t, hd - 2 * half), jnp.float32)
    z16 = jnp.zeros((t, half), jnp.float32)
    cos_t = jnp.concatenate([cos, cos, jnp.ones_like(zeros)], axis=1)
    sa_t = jnp.concatenate([z16, sin, zeros], axis=1)
    sb_t = jnp.concatenate([-sin, z16, zeros], axis=1)
    return cos_t, sa_t, sb_t


def kernel(x, norm_mix_g, w_in, w_gate_up, b_gate, gla_norm_g, q_norm_g, k_norm_g, w_out, norm_ffn_g,
           peer_wq, peer_keys, peer_u, peer_v):
    batch, t, d = x.shape
    n = batch * t
    bf16 = jnp.bfloat16
    gla_qk_w = GLA_HEADS * GLA_DK
    gla_v_w = GLA_HEADS * GLA_DV
    lr0 = 2 * gla_qk_w + 2 * gla_v_w
    cos_t, sa_t, sb_t = _rope_tables(t, MOBA_HD)

    xcur = x.reshape(n, d)
    for l in range(w_in.shape[0]):
        wt = jnp.swapaxes(w_in[l], 0, 1)
        wlr_t = jnp.pad(wt[lr0:lr0 + GLA_GATE_RANK], ((0, LANES - GLA_GATE_RANK), (0, 0))).astype(bf16)
        wup_pad = jnp.pad(w_gate_up[l], ((0, LANES - GLA_GATE_RANK), (0, 0)))

        h, glr = _norm_lowrank(xcur, norm_mix_g[l][None, :], wlr_t)
        proj = _in_proj(h, wt, lr0, GLA_GATE_RANK)
        o_gla = _gla(proj, glr, wup_pad, b_gate[l][None, :], gla_norm_g[l][None, :], batch, t)
        o_moba, u_b, v_b = _moba(proj, q_norm_g[l][None, :], k_norm_g[l][None, :], cos_t, sa_t, sb_t, batch, t, lr0,
                                 peer_u[l], peer_v[l])
        x1 = _out_proj(xcur, o_gla, o_moba, w_out[l])

        xn, st = _peer_q(x1, norm_ffn_g[l][None, :], peer_wq[l].astype(bf16), peer_keys[l])
        s1m, e1, s2m, e2n, tau = _peer_topk(st)
        xcur = _peer_ffn(x1, xn, u_b, v_b, s1m, e1, s2m, e2n, tau)
    return xcur.reshape(batch, t, d)
```
